```python
import math
import jax
import jax.numpy as jnp
from jax import lax
import numpy as np

D_MODEL = 2048
BATCH = 2
SEQ = 8192
DEPTH = 1

HEAD_DIM = 128
N_Q_HEADS = D_MODEL // HEAD_DIM
N_KV_HEADS = N_Q_HEADS // 4
Q_BLOCK = 128
ROPE_THETA = 10000.0
ROPE_AXIS_DIM = HEAD_DIM // 2
GRID_W = 64
POOL_WIDTH = D_MODEL
POOL_WINDOWS = (2, 4, 8, 16)
N_POOL_GROUPS = len(POOL_WINDOWS)
POOL_GROUP_W = POOL_WIDTH // N_POOL_GROUPS
Q_W = N_Q_HEADS * HEAD_DIM
KV_W = N_KV_HEADS * HEAD_DIM
IN_SPLITS = (Q_W, Q_W + KV_W, Q_W + 2 * KV_W, Q_W + 2 * KV_W + POOL_WIDTH, Q_W + 2 * KV_W + POOL_WIDTH + D_MODEL)
IN_W = Q_W + 2 * KV_W + POOL_WIDTH + 2 * D_MODEL
N_EXPERTS = 64
TOP_K = 6
N_EXPERT_GROUPS = 8
TOPK_GROUPS = 4
EXPERTS_PER_GROUP = N_EXPERTS // N_EXPERT_GROUPS
EXPERT_FF = (11 * D_MODEL) // 16
SHARED_FF = EXPERT_FF
ROUTED_SCALE = 2.5
MOE_BLOCK = 256
NORM_EPS = 1e-6
N_MOD = 6

kernel_name = "hybrid_gqa_pool_moe_encoder_block"


def rms_norm(x, g):
    xf = x.astype(jnp.float32)
    y = xf * lax.rsqrt(jnp.mean(xf * xf, axis=-1, keepdims=True) + NORM_EPS)
    return (y * g.astype(jnp.float32)).astype(x.dtype)


def axial_rope_tables(seq_len):
    rows = seq_len // GRID_W
    row = jnp.broadcast_to(jnp.arange(rows, dtype=jnp.float32)[:, None], (rows, GRID_W)).reshape(seq_len)
    col = jnp.broadcast_to(jnp.arange(GRID_W, dtype=jnp.float32)[None, :], (rows, GRID_W)).reshape(seq_len)
    inv_freq = ROPE_THETA ** (-jnp.arange(0, ROPE_AXIS_DIM, 2, dtype=jnp.float32) / ROPE_AXIS_DIM)
    ang_r = row[:, None] * inv_freq[None, :]
    ang_c = col[:, None] * inv_freq[None, :]
    return jnp.cos(ang_r), jnp.sin(ang_r), jnp.cos(ang_c), jnp.sin(ang_c)


def _rotate(xh, cos, sin):
    x1, x2 = jnp.split(xh, 2, axis=-1)
    cos = cos[None, :, None, :]
    sin = sin[None, :, None, :]
    return jnp.concatenate([x1 * cos - x2 * sin, x2 * cos + x1 * sin], axis=-1)


def apply_axial_rope(x, tables):
    cos_r, sin_r, cos_c, sin_c = tables
    xf = x.astype(jnp.float32)
    xr, xc = jnp.split(xf, 2, axis=-1)
    out = jnp.concatenate([_rotate(xr, cos_r, sin_r), _rotate(xc, cos_c, sin_c)], axis=-1)
    return out.astype(x.dtype)


def gqa_attention(q, k, v):
    b, s = q.shape[0], q.shape[1]
    grp = N_Q_HEADS // N_KV_HEADS
    nqb = s // Q_BLOCK
    qb = q.reshape(b, nqb, Q_BLOCK, N_KV_HEADS, grp, HEAD_DIM).transpose(1, 0, 2, 3, 4, 5)
    scale = HEAD_DIM ** -0.5

    def one_block(qblk):
        sc = jnp.einsum('bqkgd,bskd->bkgqs', qblk, k, preferred_element_type=jnp.float32) * scale
        p = jax.nn.softmax(sc, axis=-1).astype(v.dtype)
        return jnp.einsum('bkgqs,bskd->bqkgd', p, v)

    o = lax.map(one_block, qb)
    return o.transpose(1, 0, 2, 3, 4, 5).reshape(b, s, Q_W)


def multiscale_pool(p, w_pool, pool_scale):
    b, s, _ = p.shape
    pg = p.reshape(b, s, N_POOL_GROUPS, POOL_GROUP_W).astype(jnp.float32)
    cs = jnp.concatenate([jnp.zeros((b, 1, N_POOL_GROUPS, POOL_GROUP_W), jnp.float32),
                          jnp.cumsum(pg, axis=1)], axis=1)
    t = jnp.arange(s)
    outs = []
    for gi, w in enumerate(POOL_WINDOWS):
        lo = jnp.clip(t - w // 2, 0, s)
        hi = jnp.clip(t - w // 2 + w, 0, s)
        csg = cs[:, :, gi]
        mean = (jnp.take(csg, hi, axis=1) - jnp.take(csg, lo, axis=1)) / (hi - lo).astype(jnp.float32)[None, :, None]
        outs.append(mean - pg[:, :, gi])
    pooled = jnp.stack(outs, axis=2).astype(p.dtype)
    mixed = jnp.einsum('bsgc,gcd->bsgd', pooled, w_pool).reshape(b, s, POOL_WIDTH)
    return mixed * pool_scale


def token_mixer(h, w_in, q_norm_g, k_norm_g, w_pool, pool_scale, w_out, rope_tables):
    b, s, _ = h.shape
    proj = h @ w_in
    q, k, v, p, ga, gp = jnp.split(proj, IN_SPLITS, axis=-1)
    q = apply_axial_rope(rms_norm(q.reshape(b, s, N_Q_HEADS, HEAD_DIM), q_norm_g), rope_tables)
    k = apply_axial_rope(rms_norm(k.reshape(b, s, N_KV_HEADS, HEAD_DIM), k_norm_g), rope_tables)
    v = v.reshape(b, s, N_KV_HEADS, HEAD_DIM)
    attn = gqa_attention(q, k, v)
    pool = multiscale_pool(p, w_pool, pool_scale)
    merged = jax.nn.sigmoid(ga) * attn + jax.nn.sigmoid(gp) * pool
    return merged @ w_out


def moe_ffn(h, w_router, router_bias, w_gate, w_up, w_down, ws_gate, ws_up, ws_down):
    b, s, d = h.shape
    n = b * s
    xf = h.reshape(n, d)
    scores = jax.nn.sigmoid(xf.astype(jnp.float32) @ w_router.astype(jnp.float32))
    choice = scores + router_bias.astype(jnp.float32)
    grp_scores = lax.top_k(choice.reshape(n, N_EXPERT_GROUPS, EXPERTS_PER_GROUP), 2)[0].sum(-1)
    gidx = lax.top_k(grp_scores, TOPK_GROUPS)[1]
    gmask = jax.nn.one_hot(gidx, N_EXPERT_GROUPS, dtype=jnp.float32).sum(1) > 0
    emask = jnp.repeat(gmask, EXPERTS_PER_GROUP, axis=1)
    idx = lax.top_k(jnp.where(emask, choice, -jnp.inf), TOP_K)[1]
    sel = jnp.take_along_axis(scores, idx, axis=1)
    wts = sel / jnp.sum(sel, axis=-1, keepdims=True) * ROUTED_SCALE

    a = n * TOP_K
    flat_e = idx.reshape(a).astype(jnp.int32)
    flat_tok = jnp.repeat(jnp.arange(n, dtype=jnp.int32), TOP_K)
    flat_w = wts.reshape(a)
    order = jnp.argsort(flat_e)
    se, stok, sw = flat_e[order], flat_tok[order], flat_w[order]
    counts = jnp.bincount(se, length=N_EXPERTS).astype(jnp.int32)
    start = jnp.cumsum(counts) - counts
    padded = ((counts + MOE_BLOCK - 1) // MOE_BLOCK) * MOE_BLOCK
    pend = jnp.cumsum(padded)
    pstart = pend - padded
    dest = pstart[se] + (jnp.arange(a, dtype=jnp.int32) - start[se])
    n_rows = ((a + N_EXPERTS * MOE_BLOCK + MOE_BLOCK - 1) // MOE_BLOCK) * MOE_BLOCK
    n_blocks = n_rows // MOE_BLOCK
    row_tok = jnp.zeros((n_rows,), jnp.int32).at[dest].set(stok)
    row_w = jnp.zeros((n_rows,), jnp.float32).at[dest].set(sw)
    blk_e = jnp.clip(jnp.searchsorted(pend, jnp.arange(n_blocks, dtype=jnp.int32) * MOE_BLOCK, side='right'),
                     0, N_EXPERTS - 1).astype(jnp.int32)

    def expert_block(args):
        rows, e = args
        xb = xf[rows]
        return (jax.nn.silu(xb @ w_gate[e]) * (xb @ w_up[e])) @ w_down[e]

    yr = lax.map(expert_block, (row_tok.reshape(n_blocks, MOE_BLOCK), blk_e)).reshape(n_rows, d)
    routed = jax.ops.segment_sum(yr.astype(jnp.float32) * row_w[:, None], row_tok, num_segments=n)
    shared = (jax.nn.silu(xf @ ws_gate) * (xf @ ws_up)) @ ws_down
    return (routed.astype(h.dtype) + shared).reshape(b, s, d)


def setup_inputs(seed: int = 0) -> dict:
    key = jax.random.key(seed)
    ks = jax.random.split(key, 24)
    f32 = jnp.float32
    nrm = lambda k, shape, sc: jax.random.normal(k, shape, f32) * sc
    L, D = DEPTH, D_MODEL
    return {
        "x": nrm(ks[0], (BATCH, SEQ, D), 1.0),
        "c": nrm(ks[1], (BATCH, D), 1.0),
        "w_mod": nrm(ks[2], (L, D, N_MOD * D), 0.5 * D ** -0.5),
        "b_mod": nrm(ks[3], (L, N_MOD * D), 0.01),
        "g_pre_mix": 1.0 + nrm(ks[4], (L, D), 0.05),
        "g_post_mix": 1.0 + nrm(ks[5], (L, D), 0.05),
        "g_pre_ffn": 1.0 + nrm(ks[6], (L, D), 0.05),
        "g_post_ffn": 1.0 + nrm(ks[7], (L, D), 0.05),
        "w_in": nrm(ks[8], (L, D, IN_W), D ** -0.5),
        "q_norm_g": 1.0 + nrm(ks[9], (L, HEAD_DIM), 0.05),
        "k_norm_g": 1.0 + nrm(ks[10], (L, HEAD_DIM), 0.05),
        "w_pool": nrm(ks[11], (L, N_POOL_GROUPS, POOL_GROUP_W, POOL_GROUP_W), POOL_GROUP_W ** -0.5),
        "pool_scale": 1.0 + nrm(ks[12], (L, POOL_WIDTH), 0.1),
        "w_out": nrm(ks[13], (L, D, D), D ** -0.5),
        "w_router": nrm(ks[14], (L, D, N_EXPERTS), D ** -0.5),
        "router_bias": nrm(ks[15], (L, N_EXPERTS), 0.01),
        "w_exp_gate": nrm(ks[16], (L, N_EXPERTS, D, EXPERT_FF), D ** -0.5),
        "w_exp_up": nrm(ks[17], (L, N_EXPERTS, D, EXPERT_FF), D ** -0.5),
        "w_exp_down": nrm(ks[18], (L, N_EXPERTS, EXPERT_FF, D), EXPERT_FF ** -0.5),
        "w_sh_gate": nrm(ks[19], (L, D, SHARED_FF), D ** -0.5),
        "w_sh_up": nrm(ks[20], (L, D, SHARED_FF), D ** -0.5),
        "w_sh_down": nrm(ks[21], (L, SHARED_FF, D), SHARED_FF ** -0.5),
    }


def reference(x, c, w_mod, b_mod, g_pre_mix, g_post_mix, g_pre_ffn, g_post_ffn, w_in, q_norm_g, k_norm_g,
              w_pool, pool_scale, w_out, w_router, router_bias, w_exp_gate, w_exp_up, w_exp_down,
              w_sh_gate, w_sh_up, w_sh_down):
    rope_tables = axial_rope_tables(x.shape[1])
    for l in range(DEPTH):
        mod = (jax.nn.silu(c.astype(jnp.float32)) @ w_mod[l].astype(jnp.float32)
               + b_mod[l].astype(jnp.float32)).astype(x.dtype)
        sh_a, sc_a, gt_a, sh_f, sc_f, gt_f = jnp.split(mod[:, None, :], N_MOD, axis=-1)
        h = rms_norm(x, g_pre_mix[l]) * (1 + sc_a) + sh_a
        y = token_mixer(h, w_in[l], q_norm_g[l], k_norm_g[l], w_pool[l], pool_scale[l], w_out[l], rope_tables)
        x = x + gt_a * rms_norm(y, g_post_mix[l])
        h = rms_norm(x, g_pre_ffn[l]) * (1 + sc_f) + sh_f
        y = moe_ffn(h, w_router[l], router_bias[l], w_exp_gate[l], w_exp_up[l], w_exp_down[l],
                    w_sh_gate[l], w_sh_up[l], w_sh_down[l])
        x = x + gt_f * rms_norm(y, g_post_ffn[l])
    return x
```

```python
import functools
import math

import jax
import jax.numpy as jnp
from jax import lax
from jax.experimental import pallas as pl
from jax.experimental.pallas import tpu as pltpu

F32 = jnp.float32
BF16 = jnp.bfloat16

NORM_EPS = 1e-6
HEAD_DIM = 128
KV_GROUP = 4
GRID_W = 64
ROPE_THETA = 10000.0
ROPE_AXIS_DIM = HEAD_DIM // 2
POOL_WINDOWS = (2, 4, 8, 16)
POOL_HALO = 8
N_EXPERTS = 64
TOP_K = 6
N_EXPERT_GROUPS = 8
TOPK_GROUPS = 4
EXPERTS_PER_GROUP = N_EXPERTS // N_EXPERT_GROUPS
ROUTED_SCALE = 2.5
MOE_BLOCK = 256
N_MOD = 6
LANES = 128
SUBLANES = 8
VMEM_LIMIT = 56 * 1024 * 1024

Q_PRESCALE = (HEAD_DIM ** -0.5) * math.log2(math.e)


def _rms(x):
    return x * lax.rsqrt(jnp.mean(x * x, axis=-1, keepdims=True) + NORM_EPS)


def _params(sem, vmem=VMEM_LIMIT):
    return pltpu.CompilerParams(dimension_semantics=sem, vmem_limit_bytes=vmem)


def _mod_kernel(ct_ref, w_ref, b_ref, o_ref):
    ct = ct_ref[...]
    s = ct * jax.nn.sigmoid(ct)
    w = w_ref[...]
    rows = [jnp.sum(w * s[:, b:b + 1], axis=0, keepdims=True) for b in range(ct.shape[1])]
    o_ref[...] = jnp.concatenate(rows, axis=0) + b_ref[...]


def _mod(c, w_mod, b_mod):
    bsz, d = c.shape
    n = w_mod.shape[1]
    tn = min(512, n)
    return pl.pallas_call(
        _mod_kernel,
        grid=(n // tn,),
        in_specs=[pl.BlockSpec((d, bsz), lambda j: (0, 0)),
                  pl.BlockSpec((d, tn), lambda j: (0, j)),
                  pl.BlockSpec((1, tn), lambda j: (0, j))],
        out_specs=pl.BlockSpec((bsz, tn), lambda j: (0, j)),
        out_shape=jax.ShapeDtypeStruct((bsz, n), F32),
        compiler_params=_params(("arbitrary",)),
        name="mod",
    )(c.T, w_mod, b_mod.reshape(1, n))


def _rope_tables(seq_len):
    t = jnp.arange(seq_len)
    row = (t // GRID_W).astype(F32)
    col = (t % GRID_W).astype(F32)
    inv_freq = ROPE_THETA ** (-jnp.arange(0, ROPE_AXIS_DIM, 2, dtype=F32) / ROPE_AXIS_DIM)
    ang_r = row[:, None] * inv_freq[None, :]
    ang_c = col[:, None] * inv_freq[None, :]
    cr, sr, cc, sc = jnp.cos(ang_r), jnp.sin(ang_r), jnp.cos(ang_c), jnp.sin(ang_c)
    z = jnp.zeros_like(sr)
    cos = jnp.concatenate([cr, cr, cc, cc], axis=1)
    sa = jnp.concatenate([-sr, z, -sc, z], axis=1)
    sb = jnp.concatenate([z, sr, z, sc], axis=1)
    return cos, sa, sb


def _inproj_kernel(x_ref, g_ref, sc_ref, sh_ref, w_ref, gq_ref, gk_ref, cos_ref, sa_ref, sb_ref,
                   qkv_ref, f32_ref, h_scr, *, nq):
    j = pl.program_id(1)

    @pl.when(j == 0)
    def _():
        h = _rms(x_ref[...]) * g_ref[...]
        h = h * (1.0 + sc_ref[0]) + sh_ref[0]
        h_scr[...] = h.astype(BF16)

    acc = jnp.dot(h_scr[...], w_ref[...], preferred_element_type=F32)
    heads = acc.shape[1] // HEAD_DIM

    def qk_epilogue(gain):
        cos, sa, sb = cos_ref[...], sa_ref[...], sb_ref[...]
        for hh in range(heads):
            sl = slice(hh * HEAD_DIM, (hh + 1) * HEAD_DIM)
            y = _rms(acc[:, sl]) * gain
            r = y * cos + pltpu.roll(y, 3 * HEAD_DIM // 4, 1) * sa + pltpu.roll(y, HEAD_DIM // 4, 1) * sb
            qkv_ref[:, sl] = r.astype(BF16)

    @pl.when(j < nq)
    def _():
        qk_epilogue(gq_ref[...] * Q_PRESCALE)

    @pl.when(j == nq)
    def _():
        qk_epilogue(gk_ref[...])

    @pl.when(j == nq + 1)
    def _():
        qkv_ref[...] = acc.astype(BF16)

    @pl.when(j > nq + 1)
    def _():
        f32_ref[...] = acc


def _inproj(x2, g_pre, sc, sh, w_in_bf, gq, gk, seq_len):
    n, d = x2.shape
    kv_w = d // KV_GROUP
    tn = kv_w
    nq = d // tn
    n_qkv = nq + 2
    n_f32 = 3 * d // tn
    tm = min(512, seq_len)
    tiles_per_seq = seq_len // tm
    cos, sa, sb = _rope_tables(seq_len)
    row = lambda i, j: (i, 0)
    bat = lambda i, j: (i // tiles_per_seq, 0, 0)
    pos = lambda i, j: (i % tiles_per_seq, 0)
    fixed = lambda i, j: (0, 0)
    return pl.pallas_call(
        functools.partial(_inproj_kernel, nq=nq),
        grid=(n // tm, n_qkv + n_f32),
        in_specs=[pl.BlockSpec((tm, d), row),
                  pl.BlockSpec((1, d), fixed),
                  pl.BlockSpec((1, 1, d), bat),
                  pl.BlockSpec((1, 1, d), bat),
                  pl.BlockSpec((d, tn), lambda i, j: (0, j)),
                  pl.BlockSpec((1, HEAD_DIM), fixed),
                  pl.BlockSpec((1, HEAD_DIM), fixed),
                  pl.BlockSpec((tm, HEAD_DIM), pos),
                  pl.BlockSpec((tm, HEAD_DIM), pos),
                  pl.BlockSpec((tm, HEAD_DIM), pos)],
        out_specs=[pl.BlockSpec((tm, tn), lambda i, j: (i, jnp.minimum(j, n_qkv - 1))),
                   pl.BlockSpec((tm, tn), lambda i, j: (i, jnp.maximum(j - n_qkv, 0)))],
        out_shape=[jax.ShapeDtypeStruct((n, n_qkv * tn), BF16),
                   jax.ShapeDtypeStruct((n, n_f32 * tn), F32)],
        scratch_shapes=[pltpu.VMEM((tm, d), BF16)],
        compiler_params=_params(("arbitrary", "arbitrary")),
        name="inproj",
    )(x2, g_pre.reshape(1, d), sc, sh, w_in_bf, gq.reshape(1, HEAD_DIM), gk.reshape(1, HEAD_DIM),
      cos, sa, sb)


def _attn_kernel(q_ref, k_ref, v_ref, o_ref, *, tk):
    tq = q_ref.shape[0]
    seq_len = k_ref.shape[0]
    qs = jnp.concatenate([q_ref[:, g * HEAD_DIM:(g + 1) * HEAD_DIM] for g in range(KV_GROUP)], axis=0)
    rows = qs.shape[0]

    def body(c, carry):
        m, l, acc = carry
        start = pl.multiple_of(c * tk, tk)
        kc = k_ref[pl.ds(start, tk), :]
        vc = v_ref[pl.ds(start, tk), :]
        s = lax.dot_general(qs, kc, (((1,), (1,)), ((), ())), preferred_element_type=F32)
        m_new = jnp.maximum(m, jnp.max(s, axis=-1, keepdims=True))
        alpha = jnp.exp2(m - m_new)
        p = jnp.exp2(s - m_new)
        l = alpha * l + jnp.sum(p, axis=-1, keepdims=True)
        acc = alpha * acc + jnp.dot(p.astype(BF16), vc, preferred_element_type=F32)
        return m_new, l, acc

    init = (jnp.full((rows, 1), -jnp.inf, F32), jnp.zeros((rows, 1), F32), jnp.zeros((rows, HEAD_DIM), F32))
    _, l, acc = lax.fori_loop(0, seq_len // tk, body, init)
    o = acc / l
    for g in range(KV_GROUP):
        o_ref[:, g * HEAD_DIM:(g + 1) * HEAD_DIM] = o[g * tq:(g + 1) * tq].astype(BF16)


def _attention(qkv, bsz, seq_len, d):
    n = bsz * seq_len
    n_kv = d // HEAD_DIM // KV_GROUP
    gw = KV_GROUP * HEAD_DIM
    tq = min(128, seq_len)
    tk = min(512, seq_len)
    qt = seq_len // tq
    k_col0 = d // HEAD_DIM
    v_col0 = k_col0 + n_kv
    return pl.pallas_call(
        functools.partial(_attn_kernel, tk=tk),
        grid=(bsz, n_kv, qt),
        in_specs=[pl.BlockSpec((tq, gw), lambda b, h, i: (b * qt + i, h)),
                  pl.BlockSpec((seq_len, HEAD_DIM), lambda b, h, i: (b, k_col0 + h)),
                  pl.BlockSpec((seq_len, HEAD_DIM), lambda b, h, i: (b, v_col0 + h))],
        out_specs=pl.BlockSpec((tq, gw), lambda b, h, i: (b * qt + i, h)),
        out_shape=jax.ShapeDtypeStruct((n, d), BF16),
        compiler_params=_params(("arbitrary", "arbitrary", "arbitrary")),
        name="attn",
    )(qkv, qkv, qkv)


def _mix_kernel(attn_ref, p_ref, pprev_ref, pnext_ref, ga_ref, gp_ref, x_ref, gt_ref, scf_ref, shf_ref,
                wpool_ref, pscale_ref, wout_ref, gpost_ref, gpre_ref,
                x1_ref, h2_ref, pext_scr, merged_scr, *, seq_len):
    tm, d = p_ref.shape
    gw = d // len(POOL_WINDOWS)
    tiles_per_seq = seq_len // tm
    ti = pl.program_id(0) % tiles_per_seq
    pext_scr[0:POOL_HALO, :] = jnp.where(ti > 0, pprev_ref[...], 0.0)
    pext_scr[POOL_HALO:POOL_HALO + tm, :] = p_ref[...]
    pext_scr[POOL_HALO + tm:2 * POOL_HALO + tm, :] = jnp.where(ti < tiles_per_seq - 1, pnext_ref[...], 0.0)
    tpos = ti * tm + lax.broadcasted_iota(jnp.int32, (tm, 1), 0)
    for gi, w in enumerate(POOL_WINDOWS):
        cs = slice(gi * gw, (gi + 1) * gw)
        win = pext_scr[POOL_HALO - w // 2:POOL_HALO - w // 2 + tm, cs]
        for o in range(1 - w // 2, w // 2):
            win = win + pext_scr[POOL_HALO + o:POOL_HALO + o + tm, cs]
        cnt = (jnp.minimum(tpos + w // 2, seq_len) - jnp.maximum(tpos - w // 2, 0)).astype(F32)
        pooled = win / cnt - p_ref[:, cs]
        mixed = jnp.dot(pooled.astype(BF16), wpool_ref[gi], preferred_element_type=F32) * pscale_ref[:, cs]
        merged = (jax.nn.sigmoid(ga_ref[:, cs]) * attn_ref[:, cs].astype(F32)
                  + jax.nn.sigmoid(gp_ref[:, cs]) * mixed)
        merged_scr[:, cs] = merged.astype(BF16)
    y = jnp.dot(merged_scr[...], wout_ref[...], preferred_element_type=F32)
    x1 = x_ref[...] + gt_ref[0] * (_rms(y) * gpost_ref[...])
    x1_ref[...] = x1
    h2 = (_rms(x1) * gpre_ref[...]) * (1.0 + scf_ref[0]) + shf_ref[0]
    h2_ref[...] = h2.reshape(h2_ref.shape)


def _mix(attn, pgg, x2, gt_a, sc_f, sh_f, w_pool_bf, pool_scale, w_out_bf, g_post, g_pre_ffn, seq_len):
    n, d = x2.shape
    tm = min(256, seq_len)
    tiles_per_seq = seq_len // tm
    hb = tm // POOL_HALO
    n_hb = n // POOL_HALO
    row = lambda i: (i, 0)
    bat = lambda i: (i // tiles_per_seq, 0, 0)
    fixed2 = lambda i: (0, 0)
    ng = len(POOL_WINDOWS)
    gw = d // ng
    return pl.pallas_call(
        functools.partial(_mix_kernel, seq_len=seq_len),
        grid=(n // tm,),
        in_specs=[pl.BlockSpec((tm, d), row),
                  pl.BlockSpec((tm, d), lambda i: (i, 0)),
                  pl.BlockSpec((POOL_HALO, d), lambda i: (jnp.maximum(i * hb - 1, 0), 0)),
                  pl.BlockSpec((POOL_HALO, d), lambda i: (jnp.minimum((i + 1) * hb, n_hb - 1), 0)),
                  pl.BlockSpec((tm, d), lambda i: (i, 1)),
                  pl.BlockSpec((tm, d), lambda i: (i, 2)),
                  pl.BlockSpec((tm, d), row),
                  pl.BlockSpec((1, 1, d), bat),
                  pl.BlockSpec((1, 1, d), bat),
                  pl.BlockSpec((1, 1, d), bat),
                  pl.BlockSpec((ng, gw, gw), lambda i: (0, 0, 0)),
                  pl.BlockSpec((1, d), fixed2),
                  pl.BlockSpec((d, d), fixed2),
                  pl.BlockSpec((1, d), fixed2),
                  pl.BlockSpec((1, d), fixed2)],
        out_specs=[pl.BlockSpec((tm, d), row), pl.BlockSpec((tm, 1, d), lambda i: (i, 0, 0))],
        out_shape=[jax.ShapeDtypeStruct((n, d), F32), jax.ShapeDtypeStruct((n, 1, d), F32)],
        scratch_shapes=[pltpu.VMEM((tm + 2 * POOL_HALO, d), F32), pltpu.VMEM((tm, d), BF16)],
        compiler_params=_params(("arbitrary",)),
        name="mix",
    )(attn, pgg, pgg, pgg, pgg, pgg, x2, gt_a, sc_f, sh_f, w_pool_bf, pool_scale.reshape(1, d), w_out_bf,
      g_post.reshape(1, d), g_pre_ffn.reshape(1, d))


def _router_kernel(h_ref, wrt_ref, bias_ref, tri_ref, idx_ref, rank_ref, wts_ref, cnt_ref, carry_scr, h2d_scr):
    t = h_ref.shape[0]
    neg = -jnp.inf

    @pl.when(pl.program_id(0) == 0)
    def _():
        carry_scr[...] = jnp.zeros_like(carry_scr)

    h2d_scr[...] = h_ref[...].reshape(h2d_scr.shape)
    logits = lax.dot_general(wrt_ref[...], h2d_scr[...], (((1,), (1,)), ((), ())),
                             precision=lax.Precision.HIGHEST, preferred_element_type=F32)
    scores = jax.nn.sigmoid(logits)
    choice = scores + bias_ref[...]
    iota8 = lax.broadcasted_iota(jnp.int32, (SUBLANES, t), 0)
    iota_e = lax.broadcasted_iota(jnp.int32, (N_EXPERTS, t), 0)

    gs = jnp.full((N_EXPERT_GROUPS, t), neg, F32)
    for g in range(N_EXPERT_GROUPS):
        blk = choice[g * EXPERTS_PER_GROUP:(g + 1) * EXPERTS_PER_GROUP]
        m1 = jnp.max(blk, axis=0, keepdims=True)
        first = jnp.min(jnp.where(blk == m1, iota8, EXPERTS_PER_GROUP), axis=0, keepdims=True)
        m2 = jnp.max(jnp.where(iota8 == first, neg, blk), axis=0, keepdims=True)
        gs = jnp.where(iota8 == g, m1 + m2, gs)

    gmask = jnp.zeros((N_EXPERT_GROUPS, t), F32)
    for _ in range(TOPK_GROUPS):
        m = jnp.max(gs, axis=0, keepdims=True)
        first = jnp.min(jnp.where(gs == m, iota8, N_EXPERT_GROUPS), axis=0, keepdims=True)
        hit = iota8 == first
        gmask = jnp.where(hit, 1.0, gmask)
        gs = jnp.where(hit, neg, gs)

    masked = jnp.concatenate(
        [jnp.where(jnp.max(jnp.where(iota8 == g, gmask, 0.0), axis=0, keepdims=True) > 0.0,
                   choice[g * EXPERTS_PER_GROUP:(g + 1) * EXPERTS_PER_GROUP], neg)
         for g in range(N_EXPERT_GROUPS)], axis=0)

    selmask = jnp.zeros((N_EXPERTS, t), F32)
    idx_rows, sel_rows = [], []
    for _ in range(TOP_K):
        m = jnp.max(masked, axis=0, keepdims=True)
        first = jnp.min(jnp.where(masked == m, iota_e, N_EXPERTS), axis=0, keepdims=True)
        hit = iota_e == first
        idx_rows.append(first)
        sel_rows.append(jnp.sum(jnp.where(hit, scores, 0.0), axis=0, keepdims=True))
        selmask = jnp.where(hit, 1.0, selmask)
        masked = jnp.where(hit, neg, masked)
    denom = sel_rows[0]
    for r in sel_rows[1:]:
        denom = denom + r

    prefix = jnp.dot(selmask.astype(BF16), tri_ref[...], preferred_element_type=F32)
    rankfull = prefix + carry_scr[...]
    carry = carry_scr[...] + jnp.sum(selmask, axis=1, keepdims=True)
    carry_scr[...] = carry
    cnt_ref[...] = jnp.broadcast_to(carry, cnt_ref.shape).astype(jnp.int32)

    idx_out = jnp.zeros((SUBLANES, t), jnp.int32)
    rank_out = jnp.zeros((SUBLANES, t), jnp.int32)
    wts_out = jnp.zeros((SUBLANES, t), F32)
    for k in range(TOP_K):
        rk = jnp.sum(jnp.where(iota_e == idx_rows[k], rankfull, 0.0), axis=0, keepdims=True)
        idx_out = jnp.where(iota8 == k, idx_rows[k], idx_out)
        rank_out = jnp.where(iota8 == k, rk.astype(jnp.int32), rank_out)
        wts_out = jnp.where(iota8 == k, sel_rows[k] / denom * ROUTED_SCALE, wts_out)
    idx_ref[0] = idx_out
    rank_ref[0] = rank_out
    wts_ref[0] = wts_out


def _router(h2, w_router, router_bias):
    n, _, d = h2.shape
    t = MOE_BLOCK
    nt = n // t
    tri = (jnp.arange(t)[:, None] < jnp.arange(t)[None, :]).astype(BF16)
    blk3 = pl.BlockSpec((1, SUBLANES, t), lambda i: (i, 0, 0))
    return pl.pallas_call(
        _router_kernel,
        grid=(nt,),
        in_specs=[pl.BlockSpec((t, 1, d), lambda i: (i, 0, 0)),
                  pl.BlockSpec((N_EXPERTS, d), lambda i: (0, 0)),
                  pl.BlockSpec((N_EXPERTS, 1), lambda i: (0, 0)),
                  pl.BlockSpec((t, t), lambda i: (0, 0))],
        out_specs=[blk3, blk3, blk3, pl.BlockSpec((N_EXPERTS, LANES), lambda i: (0, 0))],
        out_shape=[jax.ShapeDtypeStruct((nt, SUBLANES, t), jnp.int32),
                   jax.ShapeDtypeStruct((nt, SUBLANES, t), jnp.int32),
                   jax.ShapeDtypeStruct((nt, SUBLANES, t), F32),
                   jax.ShapeDtypeStruct((N_EXPERTS, LANES), jnp.int32)],
        scratch_shapes=[pltpu.VMEM((N_EXPERTS, 1), F32), pltpu.VMEM((t, d), F32)],
        compiler_params=_params(("arbitrary",)),
        name="router",
    )(h2, w_router.T, router_bias.reshape(N_EXPERTS, 1), tri)


def _row_copy(src_ref, src_row, dst_ref, dst_row, sem):
    return pltpu.make_async_copy(src_ref.at[pl.ds(src_row, 1)], dst_ref.at[pl.ds(dst_row, 1)], sem)


def _dispatch_kernel(pstart_ref, padrow_ref, idx_ref, rank_ref, h_ref, xs_ref, zero_scr, sem):
    t = h_ref.shape[0]

    @pl.when(pl.program_id(0) == 0)
    def _():
        zero_scr[...] = jnp.zeros_like(zero_scr)
        n_zero = padrow_ref[0]

        def zero_copy(e):
            return pltpu.make_async_copy(zero_scr, xs_ref.at[pl.ds(padrow_ref[e + 1], MOE_BLOCK)], sem)

        def start(e, c):
            zero_copy(e).start()
            return c

        def wait(e, c):
            zero_copy(e).wait()
            return c

        lax.fori_loop(0, n_zero, start, 0)
        lax.fori_loop(0, n_zero, wait, 0)

    def dest(tok, k):
        return pstart_ref[idx_ref[0, k, tok]] + rank_ref[0, k, tok]

    def start(tok, c):
        for k in range(TOP_K):
            _row_copy(h_ref, tok, xs_ref, dest(tok, k), sem).start()
        return c

    def wait(tok, c):
        for k in range(TOP_K):
            _row_copy(h_ref, 0, xs_ref, 0, sem).wait()
        return c

    lax.fori_loop(0, t, start, 0)
    lax.fori_loop(0, t, wait, 0)


def _dispatch(h2, idx3, rank3, pstart, padrow, n_rows):
    n, _, d = h2.shape
    t = MOE_BLOCK
    smem3 = pl.BlockSpec((1, SUBLANES, t), lambda i, *_: (i, 0, 0), memory_space=pltpu.SMEM)
    return pl.pallas_call(
        _dispatch_kernel,
        grid_spec=pltpu.PrefetchScalarGridSpec(
            num_scalar_prefetch=2,
            grid=(n // t,),
            in_specs=[smem3, smem3, pl.BlockSpec((t, 1, d), lambda i, *_: (i, 0, 0))],
            out_specs=pl.BlockSpec(memory_space=pl.ANY),
            scratch_shapes=[pltpu.VMEM((MOE_BLOCK, 1, d), F32), pltpu.SemaphoreType.DMA(())]),
        out_shape=jax.ShapeDtypeStruct((n_rows, 1, d), F32),
        compiler_params=_params(("arbitrary",)),
        name="dispatch",
    )(pstart, padrow, idx3, rank3, h2)


def _expert_kernel(blk_e_ref, nused_ref, x_ref, wg_ref, wu_ref, wd_ref, y_ref, x2d_scr):
    del blk_e_ref

    @pl.when(pl.program_id(0) < nused_ref[0])
    def _():
        x2d_scr[...] = x_ref[...].reshape(x2d_scr.shape)
        x = x2d_scr[...].astype(BF16)
        g = jnp.dot(x, wg_ref[0], preferred_element_type=F32)
        u = jnp.dot(x, wu_ref[0], preferred_element_type=F32)
        h = (g * jax.nn.sigmoid(g)) * u
        y = jnp.dot(h.astype(BF16), wd_ref[0], preferred_element_type=F32)
        y_ref[...] = y.reshape(y_ref.shape)

    @pl.when(pl.program_id(0) >= nused_ref[0])
    def _():
        y_ref[...] = jnp.zeros_like(y_ref)


def _experts(xs, wg, wu, wd, blk_e, nused):
    n_rows, _, d = xs.shape
    ff = wg.shape[2]
    nb = n_rows // MOE_BLOCK
    rows = lambda b, be, nu: (jnp.minimum(b, nu[0] - 1), 0, 0)
    out_rows = lambda b, be, nu: (b, 0, 0)
    return pl.pallas_call(
        _expert_kernel,
        grid_spec=pltpu.PrefetchScalarGridSpec(
            num_scalar_prefetch=2,
            grid=(nb,),
            in_specs=[pl.BlockSpec((MOE_BLOCK, 1, d), rows),
                      pl.BlockSpec((1, d, ff), lambda b, be, nu: (be[b], 0, 0)),
                      pl.BlockSpec((1, d, ff), lambda b, be, nu: (be[b], 0, 0)),
                      pl.BlockSpec((1, ff, d), lambda b, be, nu: (be[b], 0, 0))],
            out_specs=pl.BlockSpec((MOE_BLOCK, 1, d), out_rows),
            scratch_shapes=[pltpu.VMEM((MOE_BLOCK, d), F32)]),
        out_shape=jax.ShapeDtypeStruct((n_rows, 1, d), F32),
        compiler_params=_params(("arbitrary",)),
        name="experts",
    )(blk_e, nused, xs, wg, wu, wd)


def _combine_kernel(pstart_ref, idx_ref, rank_ref, wts_ref, ysh_ref, x1_ref, gt_ref, gpost_ref, yr_ref,
                    out_ref, buf, row2d_scr, sem):
    t = x1_ref.shape[0]

    def copy(tok, k):
        dest = pstart_ref[idx_ref[0, k, tok]] + rank_ref[0, k, tok]
        return _row_copy(yr_ref, dest, buf.at[k], tok, sem)

    def start(tok, c):
        for k in range(TOP_K):
            copy(tok, k).start()
        return c

    def wait(tok, c):
        for k in range(TOP_K):
            _row_copy(yr_ref, 0, buf.at[k], 0, sem).wait()
        return c

    lax.fori_loop(0, t, start, 0)
    lax.fori_loop(0, t, wait, 0)
    row2d_scr[...] = ysh_ref[...].reshape(row2d_scr.shape)
    y = row2d_scr[...]
    for k in range(TOP_K):
        row2d_scr[...] = buf[k].reshape(row2d_scr.shape)
        y = y + wts_ref[:, k:k + 1] * row2d_scr[...]
    out_ref[...] = x1_ref[...] + gt_ref[0] * (_rms(y) * gpost_ref[...])


def _combine(yr, idx3, rank3, wts, ysh, x1, gt_f, g_post, pstart, seq_len):
    n, d = x1.shape
    t = MOE_BLOCK
    tiles_per_seq = seq_len // t
    smem3 = pl.BlockSpec((1, SUBLANES, t), lambda i, *_: (i, 0, 0), memory_space=pltpu.SMEM)
    row = lambda i, *_: (i, 0)
    return pl.pallas_call(
        _combine_kernel,
        grid_spec=pltpu.PrefetchScalarGridSpec(
            num_scalar_prefetch=1,
            grid=(n // t,),
            in_specs=[smem3, smem3,
                      pl.BlockSpec((t, SUBLANES), row),
                      pl.BlockSpec((t, 1, d), lambda i, *_: (i, 0, 0)),
                      pl.BlockSpec((t, d), row),
                      pl.BlockSpec((1, 1, d), lambda i, *_: (i // tiles_per_seq, 0, 0)),
                      pl.BlockSpec((1, d), lambda i, *_: (0, 0)),
                      pl.BlockSpec(memory_space=pl.ANY)],
            out_specs=pl.BlockSpec((t, d), row),
            scratch_shapes=[pltpu.VMEM((TOP_K, t, 1, d), F32), pltpu.VMEM((t, d), F32),
                            pltpu.SemaphoreType.DMA(())]),
        out_shape=jax.ShapeDtypeStruct((n, d), F32),
        compiler_params=_params(("arbitrary",)),
        name="combine",
    )(pstart, idx3, rank3, wts, ysh, x1, gt_f, g_post.reshape(1, d), yr)


def _layer(x2, c, w_mod, b_mod, g_pre_mix, g_post_mix, g_pre_ffn, g_post_ffn, w_in, q_norm_g, k_norm_g,
           w_pool, pool_scale, w_out, w_router, router_bias, w_exp_gate, w_exp_up, w_exp_down,
           w_sh_gate, w_sh_up, w_sh_down, bsz, seq_len):
    n, d = x2.shape
    mod = _mod(c, w_mod, b_mod).reshape(bsz, N_MOD, 1, d)
    sh_a, sc_a, gt_a, sh_f, sc_f, gt_f = (mod[:, i] for i in range(N_MOD))

    qkv, pgg = _inproj(x2, g_pre_mix, sc_a, sh_a, w_in.astype(BF16), q_norm_g, k_norm_g, seq_len)
    attn = _attention(qkv, bsz, seq_len, d)
    x1, h2 = _mix(attn, pgg, x2, gt_a, sc_f, sh_f, w_pool.astype(BF16), pool_scale, w_out.astype(BF16),
                  g_post_mix, g_pre_ffn, seq_len)

    idx3, rank3, wts3, cnt = _router(h2, w_router, router_bias)
    counts = cnt[:, 0]
    padded = ((counts + MOE_BLOCK - 1) // MOE_BLOCK) * MOE_BLOCK
    pend = jnp.cumsum(padded)
    pstart = (pend - padded).astype(jnp.int32)
    n_rows = n * TOP_K + N_EXPERTS * MOE_BLOCK
    n_blocks = n_rows // MOE_BLOCK
    nused = (pend[-1] // MOE_BLOCK).astype(jnp.int32).reshape(1)
    blk = jnp.minimum(jnp.arange(n_blocks, dtype=jnp.int32), nused[0] - 1)
    blk_e = jnp.clip(jnp.searchsorted(pend, blk * MOE_BLOCK, side='right'), 0, N_EXPERTS - 1).astype(jnp.int32)
    all_blocks = jnp.arange(n_blocks, dtype=jnp.int32)
    zrows = jnp.concatenate([pend.astype(jnp.int32) - MOE_BLOCK, all_blocks * MOE_BLOCK])
    zvalid = jnp.concatenate([padded > 0, all_blocks >= nused[0]])
    zrows = zrows[jnp.argsort(jnp.logical_not(zvalid), stable=True)]
    padrow = jnp.concatenate([jnp.sum(zvalid).astype(jnp.int32).reshape(1), zrows])

    xs = _dispatch(h2, idx3, rank3, pstart, padrow, n_rows)
    yr = _experts(xs, w_exp_gate.astype(BF16), w_exp_up.astype(BF16), w_exp_down.astype(BF16), blk_e, nused)
    ysh = _experts(h2, w_sh_gate.astype(BF16)[None], w_sh_up.astype(BF16)[None], w_sh_down.astype(BF16)[None],
                   jnp.zeros((n // MOE_BLOCK,), jnp.int32), jnp.full((1,), n // MOE_BLOCK, jnp.int32))
    wts = wts3.transpose(0, 2, 1).reshape(n, SUBLANES)
    return _combine(yr, idx3, rank3, wts, ysh, x1, gt_f, g_post_ffn, pstart, seq_len)


def kernel(x, c, w_mod, b_mod, g_pre_mix, g_post_mix, g_pre_ffn, g_post_ffn, w_in, q_norm_g, k_norm_g, w_pool,
           pool_scale, w_out, w_router, router_bias, w_exp_gate, w_exp_up, w_exp_down, w_sh_gate, w_sh_up,
           w_sh_down):
    bsz, seq_len, d = x.shape
    x2 = x.reshape(bsz * seq_len, d)
    for l in range(w_mod.shape[0]):
        x2 = _layer(x2, c, w_mod[l], b_mod[l], g_pre_mix[l], g_post_mix[l], g_pre_ffn[l], g_post_ffn[l],
                    w_in[l], q_norm_g[l], k_norm_g[l], w_pool[l], pool_scale[l], w_out[l], w_router[l],
                    router_bias[l], w_exp_gate[l], w_exp_up[l], w_exp_down[l], w_sh_gate[l], w_sh_up[l],
                    w_sh_down[l], bsz, seq_len)
    return x2.reshape(bsz, seq_len, d)
```

```python
import functools
import math

import jax
import jax.numpy as jnp
from jax import lax
from jax.experimental import pallas as pl
from jax.experimental.pallas import tpu as pltpu

F32 = jnp.float32
BF16 = jnp.bfloat16

NORM_EPS = 1e-6
HEAD_DIM = 128
KV_GROUP = 4
GRID_W = 64
ROPE_THETA = 10000.0
ROPE_AXIS_DIM = HEAD_DIM // 2
POOL_WINDOWS = (2, 4, 8, 16)
POOL_HALO = 8
N_EXPERTS = 64
TOP_K = 6
N_EXPERT_GROUPS = 8
TOPK_GROUPS = 4
EXPERTS_PER_GROUP = N_EXPERTS // N_EXPERT_GROUPS
ROUTED_SCALE = 2.5
MOE_BLOCK = 256
N_MOD = 6
LANES = 128
SUBLANES = 8
VMEM_LIMIT = 56 * 1024 * 1024

Q_PRESCALE = (HEAD_DIM ** -0.5) * math.log2(math.e)


def _rms(x):
    return x * lax.rsqrt(jnp.mean(x * x, axis=-1, keepdims=True) + NORM_EPS)


def _params(sem, vmem=VMEM_LIMIT):
    return pltpu.CompilerParams(dimension_semantics=sem, vmem_limit_bytes=vmem)


def _mod_kernel(ct_ref, w_ref, b_ref, o_ref):
    ct = ct_ref[...]
    s = ct * jax.nn.sigmoid(ct)
    w = w_ref[...]
    rows = [jnp.sum(w * s[:, b:b + 1], axis=0, keepdims=True) for b in range(ct.shape[1])]
    o_ref[...] = jnp.concatenate(rows, axis=0) + b_ref[...]


def _mod(c, w_mod, b_mod):
    bsz, d = c.shape
    n = w_mod.shape[1]
    tn = min(512, n)
    return pl.pallas_call(
        _mod_kernel,
        grid=(n // tn,),
        in_specs=[pl.BlockSpec((d, bsz), lambda j: (0, 0)),
                  pl.BlockSpec((d, tn), lambda j: (0, j)),
                  pl.BlockSpec((1, tn), lambda j: (0, j))],
        out_specs=pl.BlockSpec((bsz, tn), lambda j: (0, j)),
        out_shape=jax.ShapeDtypeStruct((bsz, n), F32),
        compiler_params=_params(("arbitrary",)),
        name="mod",
    )(c.T, w_mod, b_mod.reshape(1, n))


def _rope_tables(seq_len):
    t = jnp.arange(seq_len)
    row = (t // GRID_W).astype(F32)
    col = (t % GRID_W).astype(F32)
    inv_freq = ROPE_THETA ** (-jnp.arange(0, ROPE_AXIS_DIM, 2, dtype=F32) / ROPE_AXIS_DIM)
    ang_r = row[:, None] * inv_freq[None, :]
    ang_c = col[:, None] * inv_freq[None, :]
    cr, sr, cc, sc = jnp.cos(ang_r), jnp.sin(ang_r), jnp.cos(ang_c), jnp.sin(ang_c)
    z = jnp.zeros_like(sr)
    cos = jnp.concatenate([cr, cr, cc, cc], axis=1)
    sa = jnp.concatenate([-sr, z, -sc, z], axis=1)
    sb = jnp.concatenate([z, sr, z, sc], axis=1)
    return cos, sa, sb


def _inproj_kernel(x_ref, g_ref, sc_ref, sh_ref, w_ref, gq_ref, gk_ref, cos_ref, sa_ref, sb_ref,
                   qkv_ref, f32_ref, h_scr, *, nq):
    j = pl.program_id(1)

    @pl.when(j == 0)
    def _():
        h = _rms(x_ref[...]) * g_ref[...]
        h = h * (1.0 + sc_ref[0]) + sh_ref[0]
        h_scr[...] = h.astype(BF16)

    acc = jnp.dot(h_scr[...], w_ref[...], preferred_element_type=F32)
    heads = acc.shape[1] // HEAD_DIM

    def qk_epilogue(gain):
        cos, sa, sb = cos_ref[...], sa_ref[...], sb_ref[...]
        for hh in range(heads):
            sl = slice(hh * HEAD_DIM, (hh + 1) * HEAD_DIM)
            y = _rms(acc[:, sl]) * gain
            r = y * cos + pltpu.roll(y, 3 * HEAD_DIM // 4, 1) * sa + pltpu.roll(y, HEAD_DIM // 4, 1) * sb
            qkv_ref[:, sl] = r.astype(BF16)

    @pl.when(j < nq)
    def _():
        qk_epilogue(gq_ref[...] * Q_PRESCALE)

    @pl.when(j == nq)
    def _():
        qk_epilogue(gk_ref[...])

    @pl.when(j == nq + 1)
    def _():
        qkv_ref[...] = acc.astype(BF16)

    @pl.when(j > nq + 1)
    def _():
        f32_ref[...] = acc


def _inproj(x2, g_pre, sc, sh, w_in_bf, gq, gk, seq_len):
    n, d = x2.shape
    kv_w = d // KV_GROUP
    tn = kv_w
    nq = d // tn
    n_qkv = nq + 2
    n_f32 = 3 * d // tn
    tm = min(1024, seq_len)
    tiles_per_seq = seq_len // tm
    cos, sa, sb = _rope_tables(seq_len)
    row = lambda i, j: (i, 0)
    bat = lambda i, j: (i // tiles_per_seq, 0, 0)
    pos = lambda i, j: (i % tiles_per_seq, 0)
    fixed = lambda i, j: (0, 0)
    return pl.pallas_call(
        functools.partial(_inproj_kernel, nq=nq),
        grid=(n // tm, n_qkv + n_f32),
        in_specs=[pl.BlockSpec((tm, d), row),
                  pl.BlockSpec((1, d), fixed),
                  pl.BlockSpec((1, 1, d), bat),
                  pl.BlockSpec((1, 1, d), bat),
                  pl.BlockSpec((d, tn), lambda i, j: (0, j)),
                  pl.BlockSpec((1, HEAD_DIM), fixed),
                  pl.BlockSpec((1, HEAD_DIM), fixed),
                  pl.BlockSpec((tm, HEAD_DIM), pos),
                  pl.BlockSpec((tm, HEAD_DIM), pos),
                  pl.BlockSpec((tm, HEAD_DIM), pos)],
        out_specs=[pl.BlockSpec((tm, tn), lambda i, j: (i, jnp.minimum(j, n_qkv - 1))),
                   pl.BlockSpec((tm, tn), lambda i, j: (i, jnp.maximum(j - n_qkv, 0)))],
        out_shape=[jax.ShapeDtypeStruct((n, n_qkv * tn), BF16),
                   jax.ShapeDtypeStruct((n, n_f32 * tn), F32)],
        scratch_shapes=[pltpu.VMEM((tm, d), BF16)],
        compiler_params=_params(("arbitrary", "arbitrary")),
        name="inproj",
    )(x2, g_pre.reshape(1, d), sc, sh, w_in_bf, gq.reshape(1, HEAD_DIM), gk.reshape(1, HEAD_DIM),
      cos, sa, sb)


def _attn_kernel(q_ref, k_ref, v_ref, o_ref, *, tk):
    tq = q_ref.shape[0]
    seq_len = k_ref.shape[0]
    qs = jnp.concatenate([q_ref[:, g * HEAD_DIM:(g + 1) * HEAD_DIM] for g in range(KV_GROUP)], axis=0)
    rows = qs.shape[0]

    def body(c, carry):
        m, l, acc = carry
        start = pl.multiple_of(c * tk, tk)
        kc = k_ref[pl.ds(start, tk), :]
        vc = v_ref[pl.ds(start, tk), :]
        s = lax.dot_general(qs, kc, (((1,), (1,)), ((), ())), preferred_element_type=F32)
        m_new = jnp.maximum(m, jnp.max(s, axis=-1, keepdims=True))
        alpha = jnp.exp2(m - m_new)
        p = jnp.exp2(s - m_new)
        l = alpha * l + jnp.sum(p, axis=-1, keepdims=True)
        acc = alpha * acc + jnp.dot(p.astype(BF16), vc, preferred_element_type=F32)
        return m_new, l, acc

    init = (jnp.full((rows, 1), -jnp.inf, F32), jnp.zeros((rows, 1), F32), jnp.zeros((rows, HEAD_DIM), F32))
    _, l, acc = lax.fori_loop(0, seq_len // tk, body, init)
    o = acc / l
    for g in range(KV_GROUP):
        o_ref[:, g * HEAD_DIM:(g + 1) * HEAD_DIM] = o[g * tq:(g + 1) * tq].astype(BF16)


ATTN_PIPE = 4


def _attn_bounded_kernel(q_ref, k_ref, v_ref, o_ref, *s_scr, tk):
    tq = q_ref.shape[0]
    seq_len = k_ref.shape[0]
    qs = jnp.concatenate([q_ref[:, g * HEAD_DIM:(g + 1) * HEAD_DIM] for g in range(KV_GROUP)], axis=0)
    rows = qs.shape[0]
    nchunks = seq_len // tk
    ahead = ATTN_PIPE // 2

    def qk(c, dst):
        kc = k_ref[pl.ds(pl.multiple_of(c * tk, tk), tk), :]
        dst[...] = lax.dot_general(kc, qs, (((1,), (1,)), ((), ())), preferred_element_type=F32)

    def pv(c, src, l, acc):
        vc = v_ref[pl.ds(pl.multiple_of(c * tk, tk), tk), :]
        pt = jnp.exp2(src[...])
        l = l + jnp.sum(pt.reshape(tk // SUBLANES, SUBLANES, rows), axis=0)
        acc = acc + lax.dot_general(vc, pt.astype(BF16), (((0,), (0,)), ((), ())),
                                    preferred_element_type=F32)
        return l, acc

    def body(i, carry):
        l, acc = carry
        c = ATTN_PIPE * i
        for u in range(ATTN_PIPE):
            qk(jnp.minimum(c + u + ahead, nchunks - 1), s_scr[(u + ahead) % ATTN_PIPE])
            l, acc = pv(c + u, s_scr[u], l, acc)
        return l, acc

    for u in range(ahead):
        qk(u, s_scr[u])
    init = (jnp.zeros((SUBLANES, rows), F32), jnp.zeros((HEAD_DIM, rows), F32))
    l, acc = lax.fori_loop(0, nchunks // ATTN_PIPE, body, init)
    o = (acc / jnp.sum(l, axis=0, keepdims=True)).T
    for g in range(KV_GROUP):
        o_ref[:, g * HEAD_DIM:(g + 1) * HEAD_DIM] = o[g * tq:(g + 1) * tq].astype(BF16)


ATTN_SCORE_BOUND = 60.0


def _attention(qkv, gq, gk, bsz, seq_len, d):
    n = bsz * seq_len
    n_kv = d // HEAD_DIM // KV_GROUP
    gw = KV_GROUP * HEAD_DIM
    tq = min(128, seq_len)
    tk = min(512, seq_len)
    qt = seq_len // tq
    k_col0 = d // HEAD_DIM
    v_col0 = k_col0 + n_kv
    bounded_ok = (seq_len // tk) % ATTN_PIPE == 0

    def call(body, scratch):
        return pl.pallas_call(
            functools.partial(body, tk=tk),
            grid=(bsz, n_kv, qt),
            in_specs=[pl.BlockSpec((tq, gw), lambda b, h, i: (b * qt + i, h)),
                      pl.BlockSpec((seq_len, HEAD_DIM), lambda b, h, i: (b, k_col0 + h)),
                      pl.BlockSpec((seq_len, HEAD_DIM), lambda b, h, i: (b, v_col0 + h))],
            out_specs=pl.BlockSpec((tq, gw), lambda b, h, i: (b * qt + i, h)),
            out_shape=jax.ShapeDtypeStruct((n, d), BF16),
            scratch_shapes=scratch,
            compiler_params=_params(("arbitrary", "arbitrary", "arbitrary")),
            name="attn",
        )(qkv, qkv, qkv)

    online = lambda: call(_attn_kernel, [])
    if not bounded_ok:
        return online()
    bound = 1.02 * HEAD_DIM * Q_PRESCALE * jnp.max(jnp.abs(gq)) * jnp.max(jnp.abs(gk))
    bounded = lambda: call(_attn_bounded_kernel, [pltpu.VMEM((tk, KV_GROUP * tq), F32)] * ATTN_PIPE)
    return lax.cond(bound <= ATTN_SCORE_BOUND, bounded, online)


def _mix_kernel(attn_ref, p_ref, pprev_ref, pnext_ref, ga_ref, gp_ref, x_ref, gt_ref, scf_ref, shf_ref,
                wpool_ref, pscale_ref, wout_ref, gpost_ref, gpre_ref,
                x1_ref, h2_ref, pext_scr, merged_scr, *, seq_len):
    tm, d = p_ref.shape
    gw = d // len(POOL_WINDOWS)
    tiles_per_seq = seq_len // tm
    ti = pl.program_id(0) % tiles_per_seq
    pext_scr[0:POOL_HALO, :] = jnp.where(ti > 0, pprev_ref[...], 0.0)
    pext_scr[POOL_HALO:POOL_HALO + tm, :] = p_ref[...]
    pext_scr[POOL_HALO + tm:2 * POOL_HALO + tm, :] = jnp.where(ti < tiles_per_seq - 1, pnext_ref[...], 0.0)
    tpos = ti * tm + lax.broadcasted_iota(jnp.int32, (tm, 1), 0)
    for gi, w in enumerate(POOL_WINDOWS):
        cs = slice(gi * gw, (gi + 1) * gw)
        win = pext_scr[POOL_HALO - w // 2:POOL_HALO - w // 2 + tm, cs]
        for o in range(1 - w // 2, w // 2):
            win = win + pext_scr[POOL_HALO + o:POOL_HALO + o + tm, cs]
        cnt = (jnp.minimum(tpos + w // 2, seq_len) - jnp.maximum(tpos - w // 2, 0)).astype(F32)
        pooled = win / cnt - p_ref[:, cs]
        mixed = jnp.dot(pooled.astype(BF16), wpool_ref[gi], preferred_element_type=F32) * pscale_ref[:, cs]
        merged = (jax.nn.sigmoid(ga_ref[:, cs]) * attn_ref[:, cs].astype(F32)
                  + jax.nn.sigmoid(gp_ref[:, cs]) * mixed)
        merged_scr[:, cs] = merged.astype(BF16)
    y = jnp.dot(merged_scr[...], wout_ref[...], preferred_element_type=F32)
    x1 = x_ref[...] + gt_ref[0] * (_rms(y) * gpost_ref[...])
    x1_ref[...] = x1
    h2 = (_rms(x1) * gpre_ref[...]) * (1.0 + scf_ref[0]) + shf_ref[0]
    h2_ref[...] = h2.reshape(h2_ref.shape)


def _mix(attn, pgg, x2, gt_a, sc_f, sh_f, w_pool_bf, pool_scale, w_out_bf, g_post, g_pre_ffn, seq_len):
    n, d = x2.shape
    tm = min(256, seq_len)
    tiles_per_seq = seq_len // tm
    hb = tm // POOL_HALO
    n_hb = n // POOL_HALO
    row = lambda i: (i, 0)
    bat = lambda i: (i // tiles_per_seq, 0, 0)
    fixed2 = lambda i: (0, 0)
    ng = len(POOL_WINDOWS)
    gw = d // ng
    return pl.pallas_call(
        functools.partial(_mix_kernel, seq_len=seq_len),
        grid=(n // tm,),
        in_specs=[pl.BlockSpec((tm, d), row),
                  pl.BlockSpec((tm, d), lambda i: (i, 0)),
                  pl.BlockSpec((POOL_HALO, d), lambda i: (jnp.maximum(i * hb - 1, 0), 0)),
                  pl.BlockSpec((POOL_HALO, d), lambda i: (jnp.minimum((i + 1) * hb, n_hb - 1), 0)),
                  pl.BlockSpec((tm, d), lambda i: (i, 1)),
                  pl.BlockSpec((tm, d), lambda i: (i, 2)),
                  pl.BlockSpec((tm, d), row),
                  pl.BlockSpec((1, 1, d), bat),
                  pl.BlockSpec((1, 1, d), bat),
                  pl.BlockSpec((1, 1, d), bat),
                  pl.BlockSpec((ng, gw, gw), lambda i: (0, 0, 0)),
                  pl.BlockSpec((1, d), fixed2),
                  pl.BlockSpec((d, d), fixed2),
                  pl.BlockSpec((1, d), fixed2),
                  pl.BlockSpec((1, d), fixed2)],
        out_specs=[pl.BlockSpec((tm, d), row), pl.BlockSpec((tm, 1, d), lambda i: (i, 0, 0))],
        out_shape=[jax.ShapeDtypeStruct((n, d), F32), jax.ShapeDtypeStruct((n, 1, d), F32)],
        scratch_shapes=[pltpu.VMEM((tm + 2 * POOL_HALO, d), F32), pltpu.VMEM((tm, d), BF16)],
        compiler_params=_params(("arbitrary",)),
        name="mix",
    )(attn, pgg, pgg, pgg, pgg, pgg, x2, gt_a, sc_f, sh_f, w_pool_bf, pool_scale.reshape(1, d), w_out_bf,
      g_post.reshape(1, d), g_pre_ffn.reshape(1, d))


def _router_kernel(h_ref, wrt_ref, bias_ref, tri_ref, idx_ref, rank_ref, wts_ref, cnt_ref, carry_scr, h2d_scr):
    t = h_ref.shape[0]
    neg = -jnp.inf

    @pl.when(pl.program_id(0) == 0)
    def _():
        carry_scr[...] = jnp.zeros_like(carry_scr)

    h2d_scr[...] = h_ref[...].reshape(h2d_scr.shape)
    logits = lax.dot_general(wrt_ref[...], h2d_scr[...], (((1,), (1,)), ((), ())),
                             precision=lax.Precision.HIGHEST, preferred_element_type=F32)
    scores = jax.nn.sigmoid(logits)
    choice = scores + bias_ref[...]
    iota8 = lax.broadcasted_iota(jnp.int32, (SUBLANES, t), 0)
    iota_e = lax.broadcasted_iota(jnp.int32, (N_EXPERTS, t), 0)

    gs = jnp.full((N_EXPERT_GROUPS, t), neg, F32)
    for g in range(N_EXPERT_GROUPS):
        blk = choice[g * EXPERTS_PER_GROUP:(g + 1) * EXPERTS_PER_GROUP]
        m1 = jnp.max(blk, axis=0, keepdims=True)
        first = jnp.min(jnp.where(blk == m1, iota8, EXPERTS_PER_GROUP), axis=0, keepdims=True)
        m2 = jnp.max(jnp.where(iota8 == first, neg, blk), axis=0, keepdims=True)
        gs = jnp.where(iota8 == g, m1 + m2, gs)

    gmask = jnp.zeros((N_EXPERT_GROUPS, t), F32)
    for _ in range(TOPK_GROUPS):
        m = jnp.max(gs, axis=0, keepdims=True)
        first = jnp.min(jnp.where(gs == m, iota8, N_EXPERT_GROUPS), axis=0, keepdims=True)
        hit = iota8 == first
        gmask = jnp.where(hit, 1.0, gmask)
        gs = jnp.where(hit, neg, gs)

    masked = jnp.concatenate(
        [jnp.where(jnp.max(jnp.where(iota8 == g, gmask, 0.0), axis=0, keepdims=True) > 0.0,
                   choice[g * EXPERTS_PER_GROUP:(g + 1) * EXPERTS_PER_GROUP], neg)
         for g in range(N_EXPERT_GROUPS)], axis=0)

    selmask = jnp.zeros((N_EXPERTS, t), F32)
    idx_rows, sel_rows = [], []
    for _ in range(TOP_K):
        m = jnp.max(masked, axis=0, keepdims=True)
        first = jnp.min(jnp.where(masked == m, iota_e, N_EXPERTS), axis=0, keepdims=True)
        hit = iota_e == first
        idx_rows.append(first)
        sel_rows.append(jnp.sum(jnp.where(hit, scores, 0.0), axis=0, keepdims=True))
        selmask = jnp.where(hit, 1.0, selmask)
        masked = jnp.where(hit, neg, masked)
    denom = sel_rows[0]
    for r in sel_rows[1:]:
        denom = denom + r

    prefix = jnp.dot(selmask.astype(BF16), tri_ref[...], preferred_element_type=F32)
    rankfull = prefix + carry_scr[...]
    carry = carry_scr[...] + jnp.sum(selmask, axis=1, keepdims=True)
    carry_scr[...] = carry
    cnt_ref[...] = jnp.broadcast_to(carry, cnt_ref.shape).astype(jnp.int32)

    idx_out = jnp.zeros((SUBLANES, t), jnp.int32)
    rank_out = jnp.zeros((SUBLANES, t), jnp.int32)
    wts_out = jnp.zeros((SUBLANES, t), F32)
    for k in range(TOP_K):
        rk = jnp.sum(jnp.where(iota_e == idx_rows[k], rankfull, 0.0), axis=0, keepdims=True)
        idx_out = jnp.where(iota8 == k, idx_rows[k], idx_out)
        rank_out = jnp.where(iota8 == k, rk.astype(jnp.int32), rank_out)
        wts_out = jnp.where(iota8 == k, sel_rows[k] / denom * ROUTED_SCALE, wts_out)
    idx_ref[0] = idx_out
    rank_ref[0] = rank_out
    wts_ref[0] = wts_out


def _router(h2, w_router, router_bias):
    n, _, d = h2.shape
    t = MOE_BLOCK
    nt = n // t
    tri = (jnp.arange(t)[:, None] < jnp.arange(t)[None, :]).astype(BF16)
    blk3 = pl.BlockSpec((1, SUBLANES, t), lambda i: (i, 0, 0))
    return pl.pallas_call(
        _router_kernel,
        grid=(nt,),
        in_specs=[pl.BlockSpec((t, 1, d), lambda i: (i, 0, 0)),
                  pl.BlockSpec((N_EXPERTS, d), lambda i: (0, 0)),
                  pl.BlockSpec((N_EXPERTS, 1), lambda i: (0, 0)),
                  pl.BlockSpec((t, t), lambda i: (0, 0))],
        out_specs=[blk3, blk3, blk3, pl.BlockSpec((N_EXPERTS, LANES), lambda i: (0, 0))],
        out_shape=[jax.ShapeDtypeStruct((nt, SUBLANES, t), jnp.int32),
                   jax.ShapeDtypeStruct((nt, SUBLANES, t), jnp.int32),
                   jax.ShapeDtypeStruct((nt, SUBLANES, t), F32),
                   jax.ShapeDtypeStruct((N_EXPERTS, LANES), jnp.int32)],
        scratch_shapes=[pltpu.VMEM((N_EXPERTS, 1), F32), pltpu.VMEM((t, d), F32)],
        compiler_params=_params(("arbitrary",)),
        name="router",
    )(h2, w_router.T, router_bias.reshape(N_EXPERTS, 1), tri)


def _row_copy(src_ref, src_row, dst_ref, dst_row, sem):
    return pltpu.make_async_copy(src_ref.at[pl.ds(src_row, 1)], dst_ref.at[pl.ds(dst_row, 1)], sem)


def _dispatch_kernel(pstart_ref, padrow_ref, idx_ref, rank_ref, h_ref, xs_ref, zero_scr, sem):
    t = h_ref.shape[0]

    @pl.when(pl.program_id(0) == 0)
    def _():
        zero_scr[...] = jnp.zeros_like(zero_scr)
        n_cand = padrow_ref.shape[0]

        def zero_copy(e):
            return pltpu.make_async_copy(zero_scr, xs_ref.at[pl.ds(jnp.maximum(padrow_ref[e], 0), MOE_BLOCK)], sem)

        def start(e, c):
            @pl.when(padrow_ref[e] >= 0)
            def _():
                zero_copy(e).start()
            return c

        def wait(e, c):
            @pl.when(padrow_ref[e] >= 0)
            def _():
                zero_copy(e).wait()
            return c

        lax.fori_loop(0, n_cand, start, 0)
        lax.fori_loop(0, n_cand, wait, 0)

    def dest(tok, k):
        return pstart_ref[idx_ref[0, k, tok]] + rank_ref[0, k, tok]

    def start(tok, c):
        for k in range(TOP_K):
            _row_copy(h_ref, tok, xs_ref, dest(tok, k), sem).start()
        return c

    def wait(tok, c):
        for k in range(TOP_K):
            _row_copy(h_ref, 0, xs_ref, 0, sem).wait()
        return c

    lax.fori_loop(0, t, start, 0)
    lax.fori_loop(0, t, wait, 0)


def _dispatch(h2, idx3, rank3, pstart, padrow, n_rows):
    n, _, d = h2.shape
    t = MOE_BLOCK
    smem3 = pl.BlockSpec((1, SUBLANES, t), lambda i, *_: (i, 0, 0), memory_space=pltpu.SMEM)
    return pl.pallas_call(
        _dispatch_kernel,
        grid_spec=pltpu.PrefetchScalarGridSpec(
            num_scalar_prefetch=2,
            grid=(n // t,),
            in_specs=[smem3, smem3, pl.BlockSpec((t, 1, d), lambda i, *_: (i, 0, 0))],
            out_specs=pl.BlockSpec(memory_space=pl.ANY),
            scratch_shapes=[pltpu.VMEM((MOE_BLOCK, 1, d), F32), pltpu.SemaphoreType.DMA(())]),
        out_shape=jax.ShapeDtypeStruct((n_rows, 1, d), F32),
        compiler_params=_params(("arbitrary",)),
        name="dispatch",
    )(pstart, padrow, idx3, rank3, h2)


def _expert_kernel(blk_e_ref, nused_ref, x_ref, wgu_ref, wd_ref, y_ref, x2d_scr):
    del blk_e_ref
    ff = wd_ref.shape[1]

    @pl.when(pl.program_id(0) < nused_ref[0])
    def _():
        x2d_scr[...] = x_ref[...].reshape(x2d_scr.shape)
        x = x2d_scr[...].astype(BF16)
        gu = jnp.dot(x, wgu_ref[0], preferred_element_type=F32)
        g, u = gu[:, :ff], gu[:, ff:]
        h = (g * jax.nn.sigmoid(g)) * u
        y = jnp.dot(h.astype(BF16), wd_ref[0], preferred_element_type=F32)
        y_ref[...] = y.reshape(y_ref.shape)

    @pl.when(pl.program_id(0) >= nused_ref[0])
    def _():
        y_ref[...] = jnp.zeros_like(y_ref)


def _experts(xs, wgu, wd, blk_e, nused):
    n_rows, _, d = xs.shape
    ff = wd.shape[1]
    nb = n_rows // MOE_BLOCK
    rows = lambda b, be, nu: (jnp.minimum(b, nu[0] - 1), 0, 0)
    out_rows = lambda b, be, nu: (b, 0, 0)
    return pl.pallas_call(
        _expert_kernel,
        grid_spec=pltpu.PrefetchScalarGridSpec(
            num_scalar_prefetch=2,
            grid=(nb,),
            in_specs=[pl.BlockSpec((MOE_BLOCK, 1, d), rows),
                      pl.BlockSpec((1, d, 2 * ff), lambda b, be, nu: (be[b], 0, 0)),
                      pl.BlockSpec((1, ff, d), lambda b, be, nu: (be[b], 0, 0))],
            out_specs=pl.BlockSpec((MOE_BLOCK, 1, d), out_rows),
            scratch_shapes=[pltpu.VMEM((MOE_BLOCK, d), F32)]),
        out_shape=jax.ShapeDtypeStruct((n_rows, 1, d), F32),
        compiler_params=_params(("arbitrary",)),
        name="experts",
    )(blk_e, nused, xs, wgu, wd)


def _combine_kernel(pstart_ref, idx_ref, rank_ref, wts_ref, ysh_ref, x1_ref, gt_ref, gpost_ref, yr_ref,
                    out_ref, buf, row2d_scr, sem):
    t = x1_ref.shape[0]

    def copy(tok, k):
        dest = pstart_ref[idx_ref[0, k, tok]] + rank_ref[0, k, tok]
        return _row_copy(yr_ref, dest, buf.at[k], tok, sem)

    def start(tok, c):
        for k in range(TOP_K):
            copy(tok, k).start()
        return c

    def wait(tok, c):
        for k in range(TOP_K):
            _row_copy(yr_ref, 0, buf.at[k], 0, sem).wait()
        return c

    lax.fori_loop(0, t, start, 0)
    lax.fori_loop(0, t, wait, 0)
    row2d_scr[...] = ysh_ref[...].reshape(row2d_scr.shape)
    y = row2d_scr[...]
    for k in range(TOP_K):
        row2d_scr[...] = buf[k].reshape(row2d_scr.shape)
        y = y + wts_ref[:, k:k + 1] * row2d_scr[...]
    out_ref[...] = x1_ref[...] + gt_ref[0] * (_rms(y) * gpost_ref[...])


def _combine(yr, idx3, rank3, wts, ysh, x1, gt_f, g_post, pstart, seq_len):
    n, d = x1.shape
    t = MOE_BLOCK
    tiles_per_seq = seq_len // t
    smem3 = pl.BlockSpec((1, SUBLANES, t), lambda i, *_: (i, 0, 0), memory_space=pltpu.SMEM)
    row = lambda i, *_: (i, 0)
    return pl.pallas_call(
        _combine_kernel,
        grid_spec=pltpu.PrefetchScalarGridSpec(
            num_scalar_prefetch=1,
            grid=(n // t,),
            in_specs=[smem3, smem3,
                      pl.BlockSpec((t, SUBLANES), row),
                      pl.BlockSpec((t, 1, d), lambda i, *_: (i, 0, 0)),
                      pl.BlockSpec((t, d), row),
                      pl.BlockSpec((1, 1, d), lambda i, *_: (i // tiles_per_seq, 0, 0)),
                      pl.BlockSpec((1, d), lambda i, *_: (0, 0)),
                      pl.BlockSpec(memory_space=pl.ANY)],
            out_specs=pl.BlockSpec((t, d), row),
            scratch_shapes=[pltpu.VMEM((TOP_K, t, 1, d), F32), pltpu.VMEM((t, d), F32),
                            pltpu.SemaphoreType.DMA(())]),
        out_shape=jax.ShapeDtypeStruct((n, d), F32),
        compiler_params=_params(("arbitrary",)),
        name="combine",
    )(pstart, idx3, rank3, wts, ysh, x1, gt_f, g_post.reshape(1, d), yr)


def _layer(x2, c, w_mod, b_mod, g_pre_mix, g_post_mix, g_pre_ffn, g_post_ffn, w_in, q_norm_g, k_norm_g,
           w_pool, pool_scale, w_out, w_router, router_bias, w_exp_gate, w_exp_up, w_exp_down,
           w_sh_gate, w_sh_up, w_sh_down, bsz, seq_len):
    n, d = x2.shape
    mod = _mod(c, w_mod, b_mod).reshape(bsz, N_MOD, 1, d)
    sh_a, sc_a, gt_a, sh_f, sc_f, gt_f = (mod[:, i] for i in range(N_MOD))

    qkv, pgg = _inproj(x2, g_pre_mix, sc_a, sh_a, w_in.astype(BF16), q_norm_g, k_norm_g, seq_len)
    attn = _attention(qkv, q_norm_g, k_norm_g, bsz, seq_len, d)
    x1, h2 = _mix(attn, pgg, x2, gt_a, sc_f, sh_f, w_pool.astype(BF16), pool_scale, w_out.astype(BF16),
                  g_post_mix, g_pre_ffn, seq_len)

    idx3, rank3, wts3, cnt = _router(h2, w_router, router_bias)
    counts = cnt[:, 0]
    padded = ((counts + MOE_BLOCK - 1) // MOE_BLOCK) * MOE_BLOCK
    pend = jnp.cumsum(padded)
    pstart = (pend - padded).astype(jnp.int32)
    n_rows = n * TOP_K + N_EXPERTS * MOE_BLOCK
    n_blocks = n_rows // MOE_BLOCK
    nused = (pend[-1] // MOE_BLOCK).astype(jnp.int32).reshape(1)
    blk = jnp.minimum(jnp.arange(n_blocks, dtype=jnp.int32), nused[0] - 1)
    blk_e = jnp.sum(pend[None, :] <= (blk * MOE_BLOCK)[:, None], axis=1).astype(jnp.int32)
    blk_e = jnp.minimum(blk_e, N_EXPERTS - 1)
    all_blocks = jnp.arange(n_blocks, dtype=jnp.int32)
    padrow = jnp.concatenate([jnp.where(padded > 0, pend.astype(jnp.int32) - MOE_BLOCK, -1),
                              jnp.where(all_blocks >= nused[0], all_blocks * MOE_BLOCK, -1)])

    xs = _dispatch(h2, idx3, rank3, pstart, padrow, n_rows)
    gate_up = lambda wg, wu: jnp.concatenate([wg, wu], axis=-1).astype(BF16)
    yr = _experts(xs, gate_up(w_exp_gate, w_exp_up), w_exp_down.astype(BF16), blk_e, nused)
    ysh = _experts(h2, gate_up(w_sh_gate, w_sh_up)[None], w_sh_down.astype(BF16)[None],
                   jnp.zeros((n // MOE_BLOCK,), jnp.int32), jnp.full((1,), n // MOE_BLOCK, jnp.int32))
    wts = wts3.transpose(0, 2, 1).reshape(n, SUBLANES)
    return _combine(yr, idx3, rank3, wts, ysh, x1, gt_f, g_post_ffn, pstart, seq_len)


def kernel(x, c, w_mod, b_mod, g_pre_mix, g_post_mix, g_pre_ffn, g_post_ffn, w_in, q_norm_g, k_norm_g, w_pool,
           pool_scale, w_out, w_router, router_bias, w_exp_gate, w_exp_up, w_exp_down, w_sh_gate, w_sh_up,
           w_sh_down):
    bsz, seq_len, d = x.shape
    x2 = x.reshape(bsz * seq_len, d)
    for l in range(w_mod.shape[0]):
        x2 = _layer(x2, c, w_mod[l], b_mod[l], g_pre_mix[l], g_post_mix[l], g_pre_ffn[l], g_post_ffn[l],
                    w_in[l], q_norm_g[l], k_norm_g[l], w_pool[l], pool_scale[l], w_out[l], w_router[l],
                    router_bias[l], w_exp_gate[l], w_exp_up[l], w_exp_down[l], w_sh_gate[l], w_sh_up[l],
                    w_sh_down[l], bsz, seq_len)
    return x2.reshape(bsz, seq_len, d)
```

```python
import functools
import math

import jax
import jax.numpy as jnp
from jax import lax
from jax.experimental import pallas as pl
from jax.experimental.pallas import tpu as pltpu

F32 = jnp.float32
BF16 = jnp.bfloat16

NORM_EPS = 1e-6
HEAD_DIM = 128
KV_GROUP = 4
GRID_W = 64
ROPE_THETA = 10000.0
ROPE_AXIS_DIM = HEAD_DIM // 2
POOL_WINDOWS = (2, 4, 8, 16)
POOL_HALO = 8
N_EXPERTS = 64
TOP_K = 6
N_EXPERT_GROUPS = 8
TOPK_GROUPS = 4
EXPERTS_PER_GROUP = N_EXPERTS // N_EXPERT_GROUPS
ROUTED_SCALE = 2.5
MOE_BLOCK = 256
N_MOD = 6
LANES = 128
SUBLANES = 8
VMEM_LIMIT = 56 * 1024 * 1024

Q_PRESCALE = (HEAD_DIM ** -0.5) * math.log2(math.e)


def _rms(x):
    return x * lax.rsqrt(jnp.mean(x * x, axis=-1, keepdims=True) + NORM_EPS)


def _params(sem, vmem=VMEM_LIMIT):
    return pltpu.CompilerParams(dimension_semantics=sem, vmem_limit_bytes=vmem)


def _mod_kernel(ct_ref, w_ref, b_ref, o_ref):
    ct = ct_ref[...]
    s = ct * jax.nn.sigmoid(ct)
    w = w_ref[...]
    rows = [jnp.sum(w * s[:, b:b + 1], axis=0, keepdims=True) for b in range(ct.shape[1])]
    o_ref[...] = jnp.concatenate(rows, axis=0) + b_ref[...]


def _mod(c, w_mod, b_mod):
    bsz, d = c.shape
    n = w_mod.shape[1]
    tn = min(512, n)
    return pl.pallas_call(
        _mod_kernel,
        grid=(n // tn,),
        in_specs=[pl.BlockSpec((d, bsz), lambda j: (0, 0)),
                  pl.BlockSpec((d, tn), lambda j: (0, j)),
                  pl.BlockSpec((1, tn), lambda j: (0, j))],
        out_specs=pl.BlockSpec((bsz, tn), lambda j: (0, j)),
        out_shape=jax.ShapeDtypeStruct((bsz, n), F32),
        compiler_params=_params(("arbitrary",)),
        name="mod",
    )(c.T, w_mod, b_mod.reshape(1, n))


def _rope_tables(seq_len):
    t = jnp.arange(seq_len)
    row = (t // GRID_W).astype(F32)
    col = (t % GRID_W).astype(F32)
    inv_freq = ROPE_THETA ** (-jnp.arange(0, ROPE_AXIS_DIM, 2, dtype=F32) / ROPE_AXIS_DIM)
    ang_r = row[:, None] * inv_freq[None, :]
    ang_c = col[:, None] * inv_freq[None, :]
    cr, sr, cc, sc = jnp.cos(ang_r), jnp.sin(ang_r), jnp.cos(ang_c), jnp.sin(ang_c)
    z = jnp.zeros_like(sr)
    cos = jnp.concatenate([cr, cr, cc, cc], axis=1)
    sa = jnp.concatenate([-sr, z, -sc, z], axis=1)
    sb = jnp.concatenate([z, sr, z, sc], axis=1)
    return cos, sa, sb


def _inproj_kernel(x_ref, g_ref, sc_ref, sh_ref, w_ref, gq_ref, gk_ref, cos_ref, sa_ref, sb_ref,
                   qkv_ref, f32_ref, h_scr, *, nq):
    j = pl.program_id(1)

    @pl.when(j == 0)
    def _():
        h = _rms(x_ref[...]) * g_ref[...]
        h = h * (1.0 + sc_ref[0]) + sh_ref[0]
        h_scr[...] = h.astype(BF16)

    acc = jnp.dot(h_scr[...], w_ref[...], preferred_element_type=F32)
    heads = acc.shape[1] // HEAD_DIM

    def qk_epilogue(gain):
        cos, sa, sb = cos_ref[...], sa_ref[...], sb_ref[...]
        for hh in range(heads):
            sl = slice(hh * HEAD_DIM, (hh + 1) * HEAD_DIM)
            y = _rms(acc[:, sl]) * gain
            r = y * cos + pltpu.roll(y, 3 * HEAD_DIM // 4, 1) * sa + pltpu.roll(y, HEAD_DIM // 4, 1) * sb
            qkv_ref[:, sl] = r.astype(BF16)

    @pl.when(j < nq)
    def _():
        qk_epilogue(gq_ref[...] * Q_PRESCALE)

    @pl.when(j == nq)
    def _():
        qk_epilogue(gk_ref[...])

    @pl.when(j == nq + 1)
    def _():
        qkv_ref[...] = acc.astype(BF16)

    @pl.when(j > nq + 1)
    def _():
        f32_ref[...] = acc


def _inproj(x2, g_pre, sc, sh, w_in_bf, gq, gk, seq_len):
    n, d = x2.shape
    kv_w = d // KV_GROUP
    tn = kv_w
    nq = d // tn
    n_qkv = nq + 2
    n_f32 = 3 * d // tn
    tm = min(1024, seq_len)
    tiles_per_seq = seq_len // tm
    cos, sa, sb = _rope_tables(seq_len)
    row = lambda i, j: (i, 0)
    bat = lambda i, j: (i // tiles_per_seq, 0, 0)
    pos = lambda i, j: (i % tiles_per_seq, 0)
    fixed = lambda i, j: (0, 0)
    return pl.pallas_call(
        functools.partial(_inproj_kernel, nq=nq),
        grid=(n // tm, n_qkv + n_f32),
        in_specs=[pl.BlockSpec((tm, d), row),
                  pl.BlockSpec((1, d), fixed),
                  pl.BlockSpec((1, 1, d), bat),
                  pl.BlockSpec((1, 1, d), bat),
                  pl.BlockSpec((d, tn), lambda i, j: (0, j)),
                  pl.BlockSpec((1, HEAD_DIM), fixed),
                  pl.BlockSpec((1, HEAD_DIM), fixed),
                  pl.BlockSpec((tm, HEAD_DIM), pos),
                  pl.BlockSpec((tm, HEAD_DIM), pos),
                  pl.BlockSpec((tm, HEAD_DIM), pos)],
        out_specs=[pl.BlockSpec((tm, tn), lambda i, j: (i, jnp.minimum(j, n_qkv - 1))),
                   pl.BlockSpec((tm, tn), lambda i, j: (i, jnp.maximum(j - n_qkv, 0)))],
        out_shape=[jax.ShapeDtypeStruct((n, n_qkv * tn), BF16),
                   jax.ShapeDtypeStruct((n, n_f32 * tn), F32)],
        scratch_shapes=[pltpu.VMEM((tm, d), BF16)],
        compiler_params=_params(("arbitrary", "arbitrary")),
        name="inproj",
    )(x2, g_pre.reshape(1, d), sc, sh, w_in_bf, gq.reshape(1, HEAD_DIM), gk.reshape(1, HEAD_DIM),
      cos, sa, sb)


def _attn_kernel(q_ref, k_ref, v_ref, o_ref, *, tk):
    tq = q_ref.shape[0]
    seq_len = k_ref.shape[0]
    qs = jnp.concatenate([q_ref[:, g * HEAD_DIM:(g + 1) * HEAD_DIM] for g in range(KV_GROUP)], axis=0)
    rows = qs.shape[0]

    def body(c, carry):
        m, l, acc = carry
        start = pl.multiple_of(c * tk, tk)
        kc = k_ref[pl.ds(start, tk), :]
        vc = v_ref[pl.ds(start, tk), :]
        s = lax.dot_general(qs, kc, (((1,), (1,)), ((), ())), preferred_element_type=F32)
        m_new = jnp.maximum(m, jnp.max(s, axis=-1, keepdims=True))
        alpha = jnp.exp2(m - m_new)
        p = jnp.exp2(s - m_new)
        l = alpha * l + jnp.sum(p, axis=-1, keepdims=True)
        acc = alpha * acc + jnp.dot(p.astype(BF16), vc, preferred_element_type=F32)
        return m_new, l, acc

    init = (jnp.full((rows, 1), -jnp.inf, F32), jnp.zeros((rows, 1), F32), jnp.zeros((rows, HEAD_DIM), F32))
    _, l, acc = lax.fori_loop(0, seq_len // tk, body, init)
    o = acc / l
    for g in range(KV_GROUP):
        o_ref[:, g * HEAD_DIM:(g + 1) * HEAD_DIM] = o[g * tq:(g + 1) * tq].astype(BF16)


ATTN_PIPE = 4


def _attn_bounded_kernel(q_ref, k_ref, v_ref, o_ref, *s_scr, tk):
    tq = q_ref.shape[0]
    seq_len = k_ref.shape[0]
    qs = jnp.concatenate([q_ref[:, g * HEAD_DIM:(g + 1) * HEAD_DIM] for g in range(KV_GROUP)], axis=0)
    rows = qs.shape[0]
    nchunks = seq_len // tk
    ahead = ATTN_PIPE // 2

    def qk(c, dst):
        kc = k_ref[pl.ds(pl.multiple_of(c * tk, tk), tk), :]
        dst[...] = lax.dot_general(kc, qs, (((1,), (1,)), ((), ())), preferred_element_type=F32)

    def pv(c, src, l, acc):
        vc = v_ref[pl.ds(pl.multiple_of(c * tk, tk), tk), :]
        pt = jnp.exp2(src[...])
        l = l + jnp.sum(pt.reshape(tk // SUBLANES, SUBLANES, rows), axis=0)
        acc = acc + lax.dot_general(vc, pt.astype(BF16), (((0,), (0,)), ((), ())),
                                    preferred_element_type=F32)
        return l, acc

    def body(i, carry):
        l, acc = carry
        c = ATTN_PIPE * i
        for u in range(ATTN_PIPE):
            qk(jnp.minimum(c + u + ahead, nchunks - 1), s_scr[(u + ahead) % ATTN_PIPE])
            l, acc = pv(c + u, s_scr[u], l, acc)
        return l, acc

    for u in range(ahead):
        qk(u, s_scr[u])
    init = (jnp.zeros((SUBLANES, rows), F32), jnp.zeros((HEAD_DIM, rows), F32))
    l, acc = lax.fori_loop(0, nchunks // ATTN_PIPE, body, init)
    o = (acc / jnp.sum(l, axis=0, keepdims=True)).T
    for g in range(KV_GROUP):
        o_ref[:, g * HEAD_DIM:(g + 1) * HEAD_DIM] = o[g * tq:(g + 1) * tq].astype(BF16)


ATTN_SCORE_BOUND = 60.0


def _attention(qkv, gq, gk, bsz, seq_len, d):
    n = bsz * seq_len
    n_kv = d // HEAD_DIM // KV_GROUP
    gw = KV_GROUP * HEAD_DIM
    tq = min(128, seq_len)
    tk = min(512, seq_len)
    qt = seq_len // tq
    k_col0 = d // HEAD_DIM
    v_col0 = k_col0 + n_kv
    bounded_ok = (seq_len // tk) % ATTN_PIPE == 0

    def call(body, scratch):
        return pl.pallas_call(
            functools.partial(body, tk=tk),
            grid=(bsz, n_kv, qt),
            in_specs=[pl.BlockSpec((tq, gw), lambda b, h, i: (b * qt + i, h)),
                      pl.BlockSpec((seq_len, HEAD_DIM), lambda b, h, i: (b, k_col0 + h)),
                      pl.BlockSpec((seq_len, HEAD_DIM), lambda b, h, i: (b, v_col0 + h))],
            out_specs=pl.BlockSpec((tq, gw), lambda b, h, i: (b * qt + i, h)),
            out_shape=jax.ShapeDtypeStruct((n, d), BF16),
            scratch_shapes=scratch,
            compiler_params=_params(("arbitrary", "arbitrary", "arbitrary")),
            name="attn",
        )(qkv, qkv, qkv)

    online = lambda: call(_attn_kernel, [])
    if not bounded_ok:
        return online()
    bound = 1.02 * HEAD_DIM * Q_PRESCALE * jnp.max(jnp.abs(gq)) * jnp.max(jnp.abs(gk))
    bounded = lambda: call(_attn_bounded_kernel, [pltpu.VMEM((tk, KV_GROUP * tq), F32)] * ATTN_PIPE)
    return lax.cond(bound <= ATTN_SCORE_BOUND, bounded, online)


def _mix_kernel(attn_ref, p_ref, pprev_ref, pnext_ref, ga_ref, gp_ref, x_ref, gt_ref, scf_ref, shf_ref,
                wpool_ref, pscale_ref, wout_ref, gpost_ref, gpre_ref,
                x1_ref, h2_ref, pext_scr, merged_scr, *, seq_len):
    tm, d = p_ref.shape
    gw = d // len(POOL_WINDOWS)
    tiles_per_seq = seq_len // tm
    ti = pl.program_id(0) % tiles_per_seq
    pext_scr[0:POOL_HALO, :] = jnp.where(ti > 0, pprev_ref[...], 0.0)
    pext_scr[POOL_HALO:POOL_HALO + tm, :] = p_ref[...]
    pext_scr[POOL_HALO + tm:2 * POOL_HALO + tm, :] = jnp.where(ti < tiles_per_seq - 1, pnext_ref[...], 0.0)
    tpos = ti * tm + lax.broadcasted_iota(jnp.int32, (tm, 1), 0)
    for gi, w in enumerate(POOL_WINDOWS):
        cs = slice(gi * gw, (gi + 1) * gw)
        win = pext_scr[POOL_HALO - w // 2:POOL_HALO - w // 2 + tm, cs]
        for o in range(1 - w // 2, w // 2):
            win = win + pext_scr[POOL_HALO + o:POOL_HALO + o + tm, cs]
        cnt = (jnp.minimum(tpos + w // 2, seq_len) - jnp.maximum(tpos - w // 2, 0)).astype(F32)
        pooled = win / cnt - p_ref[:, cs]
        mixed = jnp.dot(pooled.astype(BF16), wpool_ref[gi], preferred_element_type=F32) * pscale_ref[:, cs]
        merged = (jax.nn.sigmoid(ga_ref[:, cs]) * attn_ref[:, cs].astype(F32)
                  + jax.nn.sigmoid(gp_ref[:, cs]) * mixed)
        merged_scr[:, cs] = merged.astype(BF16)
    y = jnp.dot(merged_scr[...], wout_ref[...], preferred_element_type=F32)
    x1 = x_ref[...] + gt_ref[0] * (_rms(y) * gpost_ref[...])
    x1_ref[...] = x1
    h2 = (_rms(x1) * gpre_ref[...]) * (1.0 + scf_ref[0]) + shf_ref[0]
    h2_ref[...] = h2.reshape(h2_ref.shape)


def _mix(attn, pgg, x2, gt_a, sc_f, sh_f, w_pool_bf, pool_scale, w_out_bf, g_post, g_pre_ffn, seq_len):
    n, d = x2.shape
    tm = min(256, seq_len)
    tiles_per_seq = seq_len // tm
    hb = tm // POOL_HALO
    n_hb = n // POOL_HALO
    row = lambda i: (i, 0)
    bat = lambda i: (i // tiles_per_seq, 0, 0)
    fixed2 = lambda i: (0, 0)
    ng = len(POOL_WINDOWS)
    gw = d // ng
    return pl.pallas_call(
        functools.partial(_mix_kernel, seq_len=seq_len),
        grid=(n // tm,),
        in_specs=[pl.BlockSpec((tm, d), row),
                  pl.BlockSpec((tm, d), lambda i: (i, 0)),
                  pl.BlockSpec((POOL_HALO, d), lambda i: (jnp.maximum(i * hb - 1, 0), 0)),
                  pl.BlockSpec((POOL_HALO, d), lambda i: (jnp.minimum((i + 1) * hb, n_hb - 1), 0)),
                  pl.BlockSpec((tm, d), lambda i: (i, 1)),
                  pl.BlockSpec((tm, d), lambda i: (i, 2)),
                  pl.BlockSpec((tm, d), row),
                  pl.BlockSpec((1, 1, d), bat),
                  pl.BlockSpec((1, 1, d), bat),
                  pl.BlockSpec((1, 1, d), bat),
                  pl.BlockSpec((ng, gw, gw), lambda i: (0, 0, 0)),
                  pl.BlockSpec((1, d), fixed2),
                  pl.BlockSpec((d, d), fixed2),
                  pl.BlockSpec((1, d), fixed2),
                  pl.BlockSpec((1, d), fixed2)],
        out_specs=[pl.BlockSpec((tm, d), row), pl.BlockSpec((tm, 1, d), lambda i: (i, 0, 0))],
        out_shape=[jax.ShapeDtypeStruct((n, d), F32), jax.ShapeDtypeStruct((n, 1, d), F32)],
        scratch_shapes=[pltpu.VMEM((tm + 2 * POOL_HALO, d), F32), pltpu.VMEM((tm, d), BF16)],
        compiler_params=_params(("arbitrary",)),
        name="mix",
    )(attn, pgg, pgg, pgg, pgg, pgg, x2, gt_a, sc_f, sh_f, w_pool_bf, pool_scale.reshape(1, d), w_out_bf,
      g_post.reshape(1, d), g_pre_ffn.reshape(1, d))


def _router_kernel(h_ref, wrt_ref, bias_ref, tri_ref, idx_ref, rank_ref, wts_ref, cnt_ref, carry_scr, h2d_scr):
    t = h_ref.shape[0]
    neg = -jnp.inf

    @pl.when(pl.program_id(0) == 0)
    def _():
        carry_scr[...] = jnp.zeros_like(carry_scr)

    h2d_scr[...] = h_ref[...].reshape(h2d_scr.shape)
    logits = lax.dot_general(wrt_ref[...], h2d_scr[...], (((1,), (1,)), ((), ())),
                             precision=lax.Precision.HIGHEST, preferred_element_type=F32)
    scores = jax.nn.sigmoid(logits)
    choice = scores + bias_ref[...]
    iota8 = lax.broadcasted_iota(jnp.int32, (SUBLANES, t), 0)
    iota_e = lax.broadcasted_iota(jnp.int32, (N_EXPERTS, t), 0)

    gs = jnp.full((N_EXPERT_GROUPS, t), neg, F32)
    for g in range(N_EXPERT_GROUPS):
        blk = choice[g * EXPERTS_PER_GROUP:(g + 1) * EXPERTS_PER_GROUP]
        m1 = jnp.max(blk, axis=0, keepdims=True)
        first = jnp.min(jnp.where(blk == m1, iota8, EXPERTS_PER_GROUP), axis=0, keepdims=True)
        m2 = jnp.max(jnp.where(iota8 == first, neg, blk), axis=0, keepdims=True)
        gs = jnp.where(iota8 == g, m1 + m2, gs)

    gmask = jnp.zeros((N_EXPERT_GROUPS, t), F32)
    for _ in range(TOPK_GROUPS):
        m = jnp.max(gs, axis=0, keepdims=True)
        first = jnp.min(jnp.where(gs == m, iota8, N_EXPERT_GROUPS), axis=0, keepdims=True)
        hit = iota8 == first
        gmask = jnp.where(hit, 1.0, gmask)
        gs = jnp.where(hit, neg, gs)

    masked = jnp.concatenate(
        [jnp.where(jnp.max(jnp.where(iota8 == g, gmask, 0.0), axis=0, keepdims=True) > 0.0,
                   choice[g * EXPERTS_PER_GROUP:(g + 1) * EXPERTS_PER_GROUP], neg)
         for g in range(N_EXPERT_GROUPS)], axis=0)

    selmask = jnp.zeros((N_EXPERTS, t), F32)
    idx_rows, sel_rows = [], []
    for _ in range(TOP_K):
        m = jnp.max(masked, axis=0, keepdims=True)
        first = jnp.min(jnp.where(masked == m, iota_e, N_EXPERTS), axis=0, keepdims=True)
        hit = iota_e == first
        idx_rows.append(first)
        sel_rows.append(jnp.sum(jnp.where(hit, scores, 0.0), axis=0, keepdims=True))
        selmask = jnp.where(hit, 1.0, selmask)
        masked = jnp.where(hit, neg, masked)
    denom = sel_rows[0]
    for r in sel_rows[1:]:
        denom = denom + r

    prefix = jnp.dot(selmask.astype(BF16), tri_ref[...], preferred_element_type=F32)
    rankfull = prefix + carry_scr[...]
    carry = carry_scr[...] + jnp.sum(selmask, axis=1, keepdims=True)
    carry_scr[...] = carry
    cnt_ref[...] = jnp.broadcast_to(carry, cnt_ref.shape).astype(jnp.int32)

    idx_out = jnp.zeros((SUBLANES, t), jnp.int32)
    rank_out = jnp.zeros((SUBLANES, t), jnp.int32)
    wts_out = jnp.zeros((SUBLANES, t), F32)
    for k in range(TOP_K):
        rk = jnp.sum(jnp.where(iota_e == idx_rows[k], rankfull, 0.0), axis=0, keepdims=True)
        idx_out = jnp.where(iota8 == k, idx_rows[k], idx_out)
        rank_out = jnp.where(iota8 == k, rk.astype(jnp.int32), rank_out)
        wts_out = jnp.where(iota8 == k, sel_rows[k] / denom * ROUTED_SCALE, wts_out)
    idx_ref[0] = idx_out
    rank_ref[0] = rank_out
    wts_ref[0] = wts_out


def _router(h2, w_router, router_bias):
    n, _, d = h2.shape
    t = MOE_BLOCK
    nt = n // t
    tri = (jnp.arange(t)[:, None] < jnp.arange(t)[None, :]).astype(BF16)
    blk3 = pl.BlockSpec((1, SUBLANES, t), lambda i: (i, 0, 0))
    return pl.pallas_call(
        _router_kernel,
        grid=(nt,),
        in_specs=[pl.BlockSpec((t, 1, d), lambda i: (i, 0, 0)),
                  pl.BlockSpec((N_EXPERTS, d), lambda i: (0, 0)),
                  pl.BlockSpec((N_EXPERTS, 1), lambda i: (0, 0)),
                  pl.BlockSpec((t, t), lambda i: (0, 0))],
        out_specs=[blk3, blk3, blk3, pl.BlockSpec((N_EXPERTS, LANES), lambda i: (0, 0))],
        out_shape=[jax.ShapeDtypeStruct((nt, SUBLANES, t), jnp.int32),
                   jax.ShapeDtypeStruct((nt, SUBLANES, t), jnp.int32),
                   jax.ShapeDtypeStruct((nt, SUBLANES, t), F32),
                   jax.ShapeDtypeStruct((N_EXPERTS, LANES), jnp.int32)],
        scratch_shapes=[pltpu.VMEM((N_EXPERTS, 1), F32), pltpu.VMEM((t, d), F32)],
        compiler_params=_params(("arbitrary",)),
        name="router",
    )(h2, w_router.T, router_bias.reshape(N_EXPERTS, 1), tri)


ROW_DMA_UNROLL = 4


def _row_copy(src_ref, src_row, dst_ref, dst_row, sem):
    return pltpu.make_async_copy(src_ref.at[pl.ds(src_row, 1)], dst_ref.at[pl.ds(dst_row, 1)], sem)


def _dispatch_kernel(padrow_ref, dest_ref, h_ref, xs_ref, zero_scr, sem):
    t = h_ref.shape[0]

    @pl.when(pl.program_id(0) == 0)
    def _():
        zero_scr[...] = jnp.zeros_like(zero_scr)
        n_cand = padrow_ref.shape[0]

        def zero_copy(e):
            return pltpu.make_async_copy(zero_scr, xs_ref.at[pl.ds(jnp.maximum(padrow_ref[e], 0), MOE_BLOCK)], sem)

        def start(e, c):
            @pl.when(padrow_ref[e] >= 0)
            def _():
                zero_copy(e).start()
            return c

        def wait(e, c):
            @pl.when(padrow_ref[e] >= 0)
            def _():
                zero_copy(e).wait()
            return c

        lax.fori_loop(0, n_cand, start, 0)
        lax.fori_loop(0, n_cand, wait, 0)

    def start(tok, c):
        for k in range(TOP_K):
            _row_copy(h_ref, tok, xs_ref, dest_ref[0, k, tok], sem).start()
        return c

    def wait(tok, c):
        for k in range(TOP_K):
            _row_copy(h_ref, 0, xs_ref, 0, sem).wait()
        return c

    lax.fori_loop(0, t, start, 0, unroll=ROW_DMA_UNROLL)
    lax.fori_loop(0, t, wait, 0, unroll=ROW_DMA_UNROLL)


def _dispatch(h2, dest3, padrow, n_rows):
    n, _, d = h2.shape
    t = MOE_BLOCK
    smem3 = pl.BlockSpec((1, SUBLANES, t), lambda i, *_: (i, 0, 0), memory_space=pltpu.SMEM)
    return pl.pallas_call(
        _dispatch_kernel,
        grid_spec=pltpu.PrefetchScalarGridSpec(
            num_scalar_prefetch=1,
            grid=(n // t,),
            in_specs=[smem3, pl.BlockSpec((t, 1, d), lambda i, *_: (i, 0, 0))],
            out_specs=pl.BlockSpec(memory_space=pl.ANY),
            scratch_shapes=[pltpu.VMEM((MOE_BLOCK, 1, d), F32), pltpu.SemaphoreType.DMA(())]),
        out_shape=jax.ShapeDtypeStruct((n_rows, 1, d), F32),
        compiler_params=_params(("arbitrary",)),
        name="dispatch",
    )(padrow, dest3, h2)


def _expert_kernel(blk_e_ref, nused_ref, x_ref, wg_ref, wu_ref, wd_ref, y_ref, x2d_scr):
    del blk_e_ref

    @pl.when(pl.program_id(0) < nused_ref[0])
    def _():
        x2d_scr[...] = x_ref[...].reshape(x2d_scr.shape)
        x = x2d_scr[...].astype(BF16)
        g = jnp.dot(x, wg_ref[0], preferred_element_type=F32)
        u = jnp.dot(x, wu_ref[0], preferred_element_type=F32)
        h = (g * jax.nn.sigmoid(g)) * u
        y = jnp.dot(h.astype(BF16), wd_ref[0], preferred_element_type=F32)
        y_ref[...] = y.reshape(y_ref.shape)

    @pl.when(pl.program_id(0) >= nused_ref[0])
    def _():
        y_ref[...] = jnp.zeros_like(y_ref)


def _experts(xs, wg, wu, wd, blk_e, nused):
    n_rows, _, d = xs.shape
    ff = wg.shape[2]
    nb = n_rows // MOE_BLOCK
    rows = lambda b, be, nu: (jnp.minimum(b, nu[0] - 1), 0, 0)
    out_rows = lambda b, be, nu: (b, 0, 0)
    return pl.pallas_call(
        _expert_kernel,
        grid_spec=pltpu.PrefetchScalarGridSpec(
            num_scalar_prefetch=2,
            grid=(nb,),
            in_specs=[pl.BlockSpec((MOE_BLOCK, 1, d), rows),
                      pl.BlockSpec((1, d, ff), lambda b, be, nu: (be[b], 0, 0)),
                      pl.BlockSpec((1, d, ff), lambda b, be, nu: (be[b], 0, 0)),
                      pl.BlockSpec((1, ff, d), lambda b, be, nu: (be[b], 0, 0))],
            out_specs=pl.BlockSpec((MOE_BLOCK, 1, d), out_rows),
            scratch_shapes=[pltpu.VMEM((MOE_BLOCK, d), F32)]),
        out_shape=jax.ShapeDtypeStruct((n_rows, 1, d), F32),
        compiler_params=_params(("arbitrary",)),
        name="experts",
    )(blk_e, nused, xs, wg, wu, wd)


NCH_GU = 16
NCH_D = 8
N_STAGE = 2


def _experts_streamed_kernel(nused_ref, blk_e_ref, first_ref, slot_ref, nxt_ref, lo_gu_ref, hi_gu_ref, lo_d_ref,
                             hi_d_ref, x_ref, wg_hbm, wu_hbm, wd_hbm, y_ref,
                             wg_bf, wu_bf, wd_bf, stg_g, stg_u, stg_d, x2d_scr, sems):
    b = pl.program_id(0)
    d, ff = wg_bf.shape[1], wg_bf.shape[2]
    ch_gu, ch_d = d // NCH_GU, ff // NCH_D
    mats = ((wg_hbm, stg_g, wg_bf, ch_gu, NCH_GU), (wu_hbm, stg_u, wu_bf, ch_gu, NCH_GU),
            (wd_hbm, stg_d, wd_bf, ch_d, NCH_D))

    def piece_copy(m, e, c):
        src, stg, _, ch, _ = mats[m]
        s = c % N_STAGE
        return pltpu.make_async_copy(src.at[e, pl.ds(pl.multiple_of(c * ch, ch), ch), :], stg.at[s], sems.at[m, s])

    def stream(m, e, slot, lo, hi):
        _, stg, dst, ch, nch = mats[m]

        def step(c, carry):
            piece_copy(m, e, c).wait()

            @pl.when(c + 1 < nch)
            def _():
                piece_copy(m, e, c + 1).start()

            dst[slot, pl.ds(pl.multiple_of(c * ch, ch), ch), :] = stg[c % N_STAGE].astype(BF16)
            return carry

        lax.fori_loop(lo, hi, step, 0)

    @pl.when(b < nused_ref[0])
    def _():
        e_next = nxt_ref[b]
        slot = slot_ref[b]
        other = 1 - slot

        @pl.when(b == 0)
        def _():
            for m in range(3):
                piece_copy(m, blk_e_ref[0], 0).start()
            for m in range(3):
                stream(m, blk_e_ref[0], slot, 0, mats[m][4])

        @pl.when(jnp.logical_and(first_ref[b] == 1, e_next >= 0))
        def _():
            for m in range(3):
                piece_copy(m, e_next, 0).start()

        def prefetch(m, lo_ref, hi_ref):
            @pl.when(e_next >= 0)
            def _():
                stream(m, e_next, other, lo_ref[b], hi_ref[b])

        x2d_scr[...] = x_ref[...].reshape(x2d_scr.shape)
        x = x2d_scr[...].astype(BF16)
        g = jnp.dot(x, wg_bf[slot], preferred_element_type=F32)
        prefetch(0, lo_gu_ref, hi_gu_ref)
        u = jnp.dot(x, wu_bf[slot], preferred_element_type=F32)
        prefetch(1, lo_gu_ref, hi_gu_ref)
        h = (g * jax.nn.sigmoid(g)) * u
        y = jnp.dot(h.astype(BF16), wd_bf[slot], preferred_element_type=F32)
        prefetch(2, lo_d_ref, hi_d_ref)
        y_ref[...] = y.reshape(y_ref.shape)

    @pl.when(b >= nused_ref[0])
    def _():
        y_ref[...] = jnp.zeros_like(y_ref)


def _stream_schedule(padded, nused, n_blocks):
    i32 = jnp.int32
    nblk = (padded // MOE_BLOCK).astype(i32)
    bend = jnp.cumsum(nblk)
    bstart = bend - nblk
    blk = jnp.minimum(jnp.arange(n_blocks, dtype=i32), nused - 1)
    blk_e = jnp.minimum(jnp.sum(bend[None, :] <= blk[:, None], axis=1), N_EXPERTS - 1).astype(i32)
    r = blk - bstart[blk_e]
    nr = jnp.maximum(nblk[blk_e], 1)
    e_ids = jnp.arange(N_EXPERTS, dtype=i32)
    live = jnp.where(nblk > 0, e_ids, N_EXPERTS)
    suffix = lax.cummin(live[::-1])[::-1]
    nxt = jnp.concatenate([suffix[1:], jnp.full((1,), N_EXPERTS, i32)])
    nxt = jnp.where(nxt >= N_EXPERTS, -1, nxt)
    ordinal = jnp.cumsum((nblk > 0).astype(i32)) - 1
    first = (r == 0).astype(i32)
    slot = (ordinal[blk_e] % 2).astype(i32)
    pieces = lambda nch: ((nch * r) // nr, (nch * (r + 1)) // nr)
    lo_gu, hi_gu = pieces(NCH_GU)
    lo_d, hi_d = pieces(NCH_D)
    return blk_e, first, slot, nxt[blk_e], lo_gu.astype(i32), hi_gu.astype(i32), lo_d.astype(i32), hi_d.astype(i32)


def _experts_streamed(xs, wg, wu, wd, padded, nused):
    n_rows, _, d = xs.shape
    ff = wg.shape[2]
    nb = n_rows // MOE_BLOCK
    sched = _stream_schedule(padded, nused[0], nb)
    rows = lambda b, nu, *_: (jnp.minimum(b, nu[0] - 1), 0, 0)
    hbm = pl.BlockSpec(memory_space=pl.ANY)
    return pl.pallas_call(
        _experts_streamed_kernel,
        grid_spec=pltpu.PrefetchScalarGridSpec(
            num_scalar_prefetch=9,
            grid=(nb,),
            in_specs=[pl.BlockSpec((MOE_BLOCK, 1, d), rows), hbm, hbm, hbm],
            out_specs=pl.BlockSpec((MOE_BLOCK, 1, d), lambda b, *_: (b, 0, 0)),
            scratch_shapes=[pltpu.VMEM((2, d, ff), BF16), pltpu.VMEM((2, d, ff), BF16), pltpu.VMEM((2, ff, d), BF16),
                            pltpu.VMEM((N_STAGE, d // NCH_GU, ff), F32), pltpu.VMEM((N_STAGE, d // NCH_GU, ff), F32),
                            pltpu.VMEM((N_STAGE, ff // NCH_D, d), F32),
                            pltpu.VMEM((MOE_BLOCK, d), F32),
                            pltpu.SemaphoreType.DMA((3, N_STAGE))]),
        out_shape=jax.ShapeDtypeStruct((n_rows, 1, d), F32),
        compiler_params=_params(("arbitrary",)),
        name="experts_routed",
    )(nused, *sched, xs, wg, wu, wd)


def _combine_kernel(dest_ref, wts_ref, ysh_ref, x1_ref, gt_ref, gpost_ref, yr_ref,
                    out_ref, buf, row2d_scr, sem):
    t = x1_ref.shape[0]

    def copy(tok, k):
        return _row_copy(yr_ref, dest_ref[0, k, tok], buf.at[k], tok, sem)

    def start(tok, c):
        for k in range(TOP_K):
            copy(tok, k).start()
        return c

    def wait(tok, c):
        for k in range(TOP_K):
            _row_copy(yr_ref, 0, buf.at[k], 0, sem).wait()
        return c

    lax.fori_loop(0, t, start, 0, unroll=ROW_DMA_UNROLL)
    lax.fori_loop(0, t, wait, 0, unroll=ROW_DMA_UNROLL)
    row2d_scr[...] = ysh_ref[...].reshape(row2d_scr.shape)
    y = row2d_scr[...]
    for k in range(TOP_K):
        row2d_scr[...] = buf[k].reshape(row2d_scr.shape)
        y = y + wts_ref[:, k:k + 1] * row2d_scr[...]
    out_ref[...] = x1_ref[...] + gt_ref[0] * (_rms(y) * gpost_ref[...])


def _combine(yr, dest3, wts, ysh, x1, gt_f, g_post, seq_len):
    n, d = x1.shape
    t = MOE_BLOCK
    tiles_per_seq = seq_len // t
    smem3 = pl.BlockSpec((1, SUBLANES, t), lambda i, *_: (i, 0, 0), memory_space=pltpu.SMEM)
    row = lambda i, *_: (i, 0)
    return pl.pallas_call(
        _combine_kernel,
        grid_spec=pltpu.PrefetchScalarGridSpec(
            num_scalar_prefetch=0,
            grid=(n // t,),
            in_specs=[smem3,
                      pl.BlockSpec((t, SUBLANES), row),
                      pl.BlockSpec((t, 1, d), lambda i, *_: (i, 0, 0)),
                      pl.BlockSpec((t, d), row),
                      pl.BlockSpec((1, 1, d), lambda i, *_: (i // tiles_per_seq, 0, 0)),
                      pl.BlockSpec((1, d), lambda i, *_: (0, 0)),
                      pl.BlockSpec(memory_space=pl.ANY)],
            out_specs=pl.BlockSpec((t, d), row),
            scratch_shapes=[pltpu.VMEM((TOP_K, t, 1, d), F32), pltpu.VMEM((t, d), F32),
                            pltpu.SemaphoreType.DMA(())]),
        out_shape=jax.ShapeDtypeStruct((n, d), F32),
        compiler_params=_params(("arbitrary",)),
        name="combine",
    )(dest3, wts, ysh, x1, gt_f, g_post.reshape(1, d), yr)


def _layer(x2, c, w_mod, b_mod, g_pre_mix, g_post_mix, g_pre_ffn, g_post_ffn, w_in, q_norm_g, k_norm_g,
           w_pool, pool_scale, w_out, w_router, router_bias, w_exp_gate, w_exp_up, w_exp_down,
           w_sh_gate, w_sh_up, w_sh_down, bsz, seq_len):
    n, d = x2.shape
    mod = _mod(c, w_mod, b_mod).reshape(bsz, N_MOD, 1, d)
    sh_a, sc_a, gt_a, sh_f, sc_f, gt_f = (mod[:, i] for i in range(N_MOD))

    qkv, pgg = _inproj(x2, g_pre_mix, sc_a, sh_a, w_in.astype(BF16), q_norm_g, k_norm_g, seq_len)
    attn = _attention(qkv, q_norm_g, k_norm_g, bsz, seq_len, d)
    x1, h2 = _mix(attn, pgg, x2, gt_a, sc_f, sh_f, w_pool.astype(BF16), pool_scale, w_out.astype(BF16),
                  g_post_mix, g_pre_ffn, seq_len)

    idx3, rank3, wts3, cnt = _router(h2, w_router, router_bias)
    counts = cnt[:, 0]
    padded = ((counts + MOE_BLOCK - 1) // MOE_BLOCK) * MOE_BLOCK
    pend = jnp.cumsum(padded)
    pstart = (pend - padded).astype(jnp.int32)
    n_rows = n * TOP_K + N_EXPERTS * MOE_BLOCK
    n_blocks = n_rows // MOE_BLOCK
    nused = (pend[-1] // MOE_BLOCK).astype(jnp.int32).reshape(1)
    all_blocks = jnp.arange(n_blocks, dtype=jnp.int32)
    padrow = jnp.concatenate([jnp.where(padded > 0, pend.astype(jnp.int32) - MOE_BLOCK, -1),
                              jnp.where(all_blocks >= nused[0], all_blocks * MOE_BLOCK, -1)])

    dest3 = jnp.take(pstart, idx3) + rank3
    xs = _dispatch(h2, dest3, padrow, n_rows)
    yr = _experts_streamed(xs, w_exp_gate, w_exp_up, w_exp_down, padded, nused)
    ysh = _experts(h2, w_sh_gate.astype(BF16)[None], w_sh_up.astype(BF16)[None], w_sh_down.astype(BF16)[None],
                   jnp.zeros((n // MOE_BLOCK,), jnp.int32), jnp.full((1,), n // MOE_BLOCK, jnp.int32))
    wts = wts3.transpose(0, 2, 1).reshape(n, SUBLANES)
    return _combine(yr, dest3, wts, ysh, x1, gt_f, g_post_ffn, seq_len)


def kernel(x, c, w_mod, b_mod, g_pre_mix, g_post_mix, g_pre_ffn, g_post_ffn, w_in, q_norm_g, k_norm_g, w_pool,
           pool_scale, w_out, w_router, router_bias, w_exp_gate, w_exp_up, w_exp_down, w_sh_gate, w_sh_up,
           w_sh_down):
    bsz, seq_len, d = x.shape
    x2 = x.reshape(bsz * seq_len, d)
    for l in range(w_mod.shape[0]):
        x2 = _layer(x2, c, w_mod[l], b_mod[l], g_pre_mix[l], g_post_mix[l], g_pre_ffn[l], g_post_ffn[l],
                    w_in[l], q_norm_g[l], k_norm_g[l], w_pool[l], pool_scale[l], w_out[l], w_router[l],
                    router_bias[l], w_exp_gate[l], w_exp_up[l], w_exp_down[l], w_sh_gate[l], w_sh_up[l],
                    w_sh_down[l], bsz, seq_len)
    return x2.reshape(bsz, seq_len, d)
```

```python
import functools
import math

import jax
import jax.numpy as jnp
from jax import lax
from jax.experimental import pallas as pl
from jax.experimental.pallas import tpu as pltpu

F32 = jnp.float32
BF16 = jnp.bfloat16

NORM_EPS = 1e-6
HEAD_DIM = 128
KV_GROUP = 4
GRID_W = 64
ROPE_THETA = 10000.0
ROPE_AXIS_DIM = HEAD_DIM // 2
POOL_WINDOWS = (2, 4, 8, 16)
POOL_HALO = 8
N_EXPERTS = 64
TOP_K = 6
N_EXPERT_GROUPS = 8
TOPK_GROUPS = 4
EXPERTS_PER_GROUP = N_EXPERTS // N_EXPERT_GROUPS
ROUTED_SCALE = 2.5
MOE_BLOCK = 256
N_MOD = 6
LANES = 128
SUBLANES = 8
VMEM_LIMIT = 56 * 1024 * 1024

Q_PRESCALE = (HEAD_DIM ** -0.5) * math.log2(math.e)


def _rms(x):
    return x * lax.rsqrt(jnp.mean(x * x, axis=-1, keepdims=True) + NORM_EPS)


def _params(sem, vmem=VMEM_LIMIT):
    return pltpu.CompilerParams(dimension_semantics=sem, vmem_limit_bytes=vmem)


def _mod_kernel(ct_ref, w_ref, b_ref, o_ref):
    ct = ct_ref[...]
    s = ct * jax.nn.sigmoid(ct)
    w = w_ref[...]
    rows = [jnp.sum(w * s[:, b:b + 1], axis=0, keepdims=True) for b in range(ct.shape[1])]
    o_ref[...] = jnp.concatenate(rows, axis=0) + b_ref[...]


def _mod(c, w_mod, b_mod):
    bsz, d = c.shape
    n = w_mod.shape[1]
    tn = min(512, n)
    return pl.pallas_call(
        _mod_kernel,
        grid=(n // tn,),
        in_specs=[pl.BlockSpec((d, bsz), lambda j: (0, 0)),
                  pl.BlockSpec((d, tn), lambda j: (0, j)),
                  pl.BlockSpec((1, tn), lambda j: (0, j))],
        out_specs=pl.BlockSpec((bsz, tn), lambda j: (0, j)),
        out_shape=jax.ShapeDtypeStruct((bsz, n), F32),
        compiler_params=_params(("arbitrary",)),
        name="mod",
    )(c.T, w_mod, b_mod.reshape(1, n))


def _rope_tables(seq_len):
    t = jnp.arange(seq_len)
    row = (t // GRID_W).astype(F32)
    col = (t % GRID_W).astype(F32)
    inv_freq = ROPE_THETA ** (-jnp.arange(0, ROPE_AXIS_DIM, 2, dtype=F32) / ROPE_AXIS_DIM)
    ang_r = row[:, None] * inv_freq[None, :]
    ang_c = col[:, None] * inv_freq[None, :]
    cr, sr, cc, sc = jnp.cos(ang_r), jnp.sin(ang_r), jnp.cos(ang_c), jnp.sin(ang_c)
    z = jnp.zeros_like(sr)
    cos = jnp.concatenate([cr, cr, cc, cc], axis=1)
    sa = jnp.concatenate([-sr, z, -sc, z], axis=1)
    sb = jnp.concatenate([z, sr, z, sc], axis=1)
    return cos, sa, sb


def _inproj_kernel(x_ref, g_ref, sc_ref, sh_ref, w_ref, gq_ref, gk_ref, cos_ref, sa_ref, sb_ref,
                   qkv_ref, f32_ref, h_scr, *, nq):
    j = pl.program_id(1)

    @pl.when(j == 0)
    def _():
        h = _rms(x_ref[...]) * g_ref[...]
        h = h * (1.0 + sc_ref[0]) + sh_ref[0]
        h_scr[...] = h.astype(BF16)

    acc = jnp.dot(h_scr[...], w_ref[...], preferred_element_type=F32)
    heads = acc.shape[1] // HEAD_DIM

    def qk_epilogue(gain):
        cos, sa, sb = cos_ref[...], sa_ref[...], sb_ref[...]
        for hh in range(heads):
            sl = slice(hh * HEAD_DIM, (hh + 1) * HEAD_DIM)
            y = _rms(acc[:, sl]) * gain
            r = y * cos + pltpu.roll(y, 3 * HEAD_DIM // 4, 1) * sa + pltpu.roll(y, HEAD_DIM // 4, 1) * sb
            qkv_ref[:, sl] = r.astype(BF16)

    @pl.when(j < nq)
    def _():
        qk_epilogue(gq_ref[...] * Q_PRESCALE)

    @pl.when(j == nq)
    def _():
        qk_epilogue(gk_ref[...])

    @pl.when(j == nq + 1)
    def _():
        qkv_ref[...] = acc.astype(BF16)

    @pl.when(j > nq + 1)
    def _():
        f32_ref[...] = acc


def _inproj(x2, g_pre, sc, sh, w_in_bf, gq, gk, seq_len):
    n, d = x2.shape
    kv_w = d // KV_GROUP
    tn = kv_w
    nq = d // tn
    n_qkv = nq + 2
    n_f32 = 3 * d // tn
    tm = min(1024, seq_len)
    tiles_per_seq = seq_len // tm
    cos, sa, sb = _rope_tables(seq_len)
    row = lambda i, j: (i, 0)
    bat = lambda i, j: (i // tiles_per_seq, 0, 0)
    pos = lambda i, j: (i % tiles_per_seq, 0)
    fixed = lambda i, j: (0, 0)
    return pl.pallas_call(
        functools.partial(_inproj_kernel, nq=nq),
        grid=(n // tm, n_qkv + n_f32),
        in_specs=[pl.BlockSpec((tm, d), row),
                  pl.BlockSpec((1, d), fixed),
                  pl.BlockSpec((1, 1, d), bat),
                  pl.BlockSpec((1, 1, d), bat),
                  pl.BlockSpec((d, tn), lambda i, j: (0, j)),
                  pl.BlockSpec((1, HEAD_DIM), fixed),
                  pl.BlockSpec((1, HEAD_DIM), fixed),
                  pl.BlockSpec((tm, HEAD_DIM), pos),
                  pl.BlockSpec((tm, HEAD_DIM), pos),
                  pl.BlockSpec((tm, HEAD_DIM), pos)],
        out_specs=[pl.BlockSpec((tm, tn), lambda i, j: (i, jnp.minimum(j, n_qkv - 1))),
                   pl.BlockSpec((tm, tn), lambda i, j: (i, jnp.maximum(j - n_qkv, 0)))],
        out_shape=[jax.ShapeDtypeStruct((n, n_qkv * tn), BF16),
                   jax.ShapeDtypeStruct((n, n_f32 * tn), F32)],
        scratch_shapes=[pltpu.VMEM((tm, d), BF16)],
        compiler_params=_params(("arbitrary", "arbitrary")),
        name="inproj",
    )(x2, g_pre.reshape(1, d), sc, sh, w_in_bf, gq.reshape(1, HEAD_DIM), gk.reshape(1, HEAD_DIM),
      cos, sa, sb)


def _attn_kernel(q_ref, k_ref, v_ref, o_ref, *, tk):
    tq = q_ref.shape[0]
    seq_len = k_ref.shape[0]
    qs = jnp.concatenate([q_ref[:, g * HEAD_DIM:(g + 1) * HEAD_DIM] for g in range(KV_GROUP)], axis=0)
    rows = qs.shape[0]

    def body(c, carry):
        m, l, acc = carry
        start = pl.multiple_of(c * tk, tk)
        kc = k_ref[pl.ds(start, tk), :]
        vc = v_ref[pl.ds(start, tk), :]
        s = lax.dot_general(qs, kc, (((1,), (1,)), ((), ())), preferred_element_type=F32)
        m_new = jnp.maximum(m, jnp.max(s, axis=-1, keepdims=True))
        alpha = jnp.exp2(m - m_new)
        p = jnp.exp2(s - m_new)
        l = alpha * l + jnp.sum(p, axis=-1, keepdims=True)
        acc = alpha * acc + jnp.dot(p.astype(BF16), vc, preferred_element_type=F32)
        return m_new, l, acc

    init = (jnp.full((rows, 1), -jnp.inf, F32), jnp.zeros((rows, 1), F32), jnp.zeros((rows, HEAD_DIM), F32))
    _, l, acc = lax.fori_loop(0, seq_len // tk, body, init)
    o = acc / l
    for g in range(KV_GROUP):
        o_ref[:, g * HEAD_DIM:(g + 1) * HEAD_DIM] = o[g * tq:(g + 1) * tq].astype(BF16)


ATTN_PIPE = 4


def _attn_bounded_kernel(q_ref, k_ref, v_ref, o_ref, *s_scr, tk):
    tq = q_ref.shape[0]
    seq_len = k_ref.shape[0]
    qs = jnp.concatenate([q_ref[:, g * HEAD_DIM:(g + 1) * HEAD_DIM] for g in range(KV_GROUP)], axis=0)
    rows = qs.shape[0]
    nchunks = seq_len // tk
    ahead = ATTN_PIPE // 2

    def qk(c, dst):
        kc = k_ref[pl.ds(pl.multiple_of(c * tk, tk), tk), :]
        dst[...] = lax.dot_general(kc, qs, (((1,), (1,)), ((), ())), preferred_element_type=F32)

    def pv(c, src, l, acc):
        vc = v_ref[pl.ds(pl.multiple_of(c * tk, tk), tk), :]
        pt = jnp.exp2(src[...])
        l = l + jnp.sum(pt.reshape(tk // SUBLANES, SUBLANES, rows), axis=0)
        acc = acc + lax.dot_general(vc, pt.astype(BF16), (((0,), (0,)), ((), ())),
                                    preferred_element_type=F32)
        return l, acc

    def body(i, carry):
        l, acc = carry
        c = ATTN_PIPE * i
        for u in range(ATTN_PIPE):
            qk(jnp.minimum(c + u + ahead, nchunks - 1), s_scr[(u + ahead) % ATTN_PIPE])
            l, acc = pv(c + u, s_scr[u], l, acc)
        return l, acc

    for u in range(ahead):
        qk(u, s_scr[u])
    init = (jnp.zeros((SUBLANES, rows), F32), jnp.zeros((HEAD_DIM, rows), F32))
    l, acc = lax.fori_loop(0, nchunks // ATTN_PIPE, body, init)
    o = (acc / jnp.sum(l, axis=0, keepdims=True)).T
    for g in range(KV_GROUP):
        o_ref[:, g * HEAD_DIM:(g + 1) * HEAD_DIM] = o[g * tq:(g + 1) * tq].astype(BF16)


ATTN_SCORE_BOUND = 60.0


def _attention(qkv, gq, gk, bsz, seq_len, d):
    n = bsz * seq_len
    n_kv = d // HEAD_DIM // KV_GROUP
    gw = KV_GROUP * HEAD_DIM
    tq = min(128, seq_len)
    tk = min(512, seq_len)
    qt = seq_len // tq
    k_col0 = d // HEAD_DIM
    v_col0 = k_col0 + n_kv
    bounded_ok = (seq_len // tk) % ATTN_PIPE == 0

    def call(body, scratch):
        return pl.pallas_call(
            functools.partial(body, tk=tk),
            grid=(bsz, n_kv, qt),
            in_specs=[pl.BlockSpec((tq, gw), lambda b, h, i: (b * qt + i, h)),
                      pl.BlockSpec((seq_len, HEAD_DIM), lambda b, h, i: (b, k_col0 + h)),
                      pl.BlockSpec((seq_len, HEAD_DIM), lambda b, h, i: (b, v_col0 + h))],
            out_specs=pl.BlockSpec((tq, gw), lambda b, h, i: (b * qt + i, h)),
            out_shape=jax.ShapeDtypeStruct((n, d), BF16),
            scratch_shapes=scratch,
            compiler_params=_params(("arbitrary", "arbitrary", "arbitrary")),
            name="attn",
        )(qkv, qkv, qkv)

    online = lambda: call(_attn_kernel, [])
    if not bounded_ok:
        return online()
    bound = 1.02 * HEAD_DIM * Q_PRESCALE * jnp.max(jnp.abs(gq)) * jnp.max(jnp.abs(gk))
    bounded = lambda: call(_attn_bounded_kernel, [pltpu.VMEM((tk, KV_GROUP * tq), F32)] * ATTN_PIPE)
    return lax.cond(bound <= ATTN_SCORE_BOUND, bounded, online)


def _mix_kernel(attn_ref, p_ref, pprev_ref, pnext_ref, ga_ref, gp_ref, x_ref, gt_ref, scf_ref, shf_ref,
                wpool_ref, pscale_ref, wout_ref, gpost_ref, gpre_ref,
                x1_ref, h2_ref, pext_scr, merged_scr, *, seq_len):
    tm, d = p_ref.shape
    gw = d // len(POOL_WINDOWS)
    tiles_per_seq = seq_len // tm
    ti = pl.program_id(0) % tiles_per_seq
    pext_scr[0:POOL_HALO, :] = jnp.where(ti > 0, pprev_ref[...], 0.0)
    pext_scr[POOL_HALO:POOL_HALO + tm, :] = p_ref[...]
    pext_scr[POOL_HALO + tm:2 * POOL_HALO + tm, :] = jnp.where(ti < tiles_per_seq - 1, pnext_ref[...], 0.0)
    tpos = ti * tm + lax.broadcasted_iota(jnp.int32, (tm, 1), 0)
    for gi, w in enumerate(POOL_WINDOWS):
        cs = slice(gi * gw, (gi + 1) * gw)
        win = pext_scr[POOL_HALO - w // 2:POOL_HALO - w // 2 + tm, cs]
        for o in range(1 - w // 2, w // 2):
            win = win + pext_scr[POOL_HALO + o:POOL_HALO + o + tm, cs]
        cnt = (jnp.minimum(tpos + w // 2, seq_len) - jnp.maximum(tpos - w // 2, 0)).astype(F32)
        pooled = win / cnt - p_ref[:, cs]
        mixed = jnp.dot(pooled.astype(BF16), wpool_ref[gi], preferred_element_type=F32) * pscale_ref[:, cs]
        merged = (jax.nn.sigmoid(ga_ref[:, cs]) * attn_ref[:, cs].astype(F32)
                  + jax.nn.sigmoid(gp_ref[:, cs]) * mixed)
        merged_scr[:, cs] = merged.astype(BF16)
    y = jnp.dot(merged_scr[...], wout_ref[...], preferred_element_type=F32)
    x1 = x_ref[...] + gt_ref[0] * (_rms(y) * gpost_ref[...])
    x1_ref[...] = x1
    h2 = (_rms(x1) * gpre_ref[...]) * (1.0 + scf_ref[0]) + shf_ref[0]
    h2_ref[...] = h2.reshape(h2_ref.shape)


def _mix(attn, pgg, x2, gt_a, sc_f, sh_f, w_pool_bf, pool_scale, w_out_bf, g_post, g_pre_ffn, seq_len):
    n, d = x2.shape
    tm = min(256, seq_len)
    tiles_per_seq = seq_len // tm
    hb = tm // POOL_HALO
    n_hb = n // POOL_HALO
    row = lambda i: (i, 0)
    bat = lambda i: (i // tiles_per_seq, 0, 0)
    fixed2 = lambda i: (0, 0)
    ng = len(POOL_WINDOWS)
    gw = d // ng
    return pl.pallas_call(
        functools.partial(_mix_kernel, seq_len=seq_len),
        grid=(n // tm,),
        in_specs=[pl.BlockSpec((tm, d), row),
                  pl.BlockSpec((tm, d), lambda i: (i, 0)),
                  pl.BlockSpec((POOL_HALO, d), lambda i: (jnp.maximum(i * hb - 1, 0), 0)),
                  pl.BlockSpec((POOL_HALO, d), lambda i: (jnp.minimum((i + 1) * hb, n_hb - 1), 0)),
                  pl.BlockSpec((tm, d), lambda i: (i, 1)),
                  pl.BlockSpec((tm, d), lambda i: (i, 2)),
                  pl.BlockSpec((tm, d), row),
                  pl.BlockSpec((1, 1, d), bat),
                  pl.BlockSpec((1, 1, d), bat),
                  pl.BlockSpec((1, 1, d), bat),
                  pl.BlockSpec((ng, gw, gw), lambda i: (0, 0, 0)),
                  pl.BlockSpec((1, d), fixed2),
                  pl.BlockSpec((d, d), fixed2),
                  pl.BlockSpec((1, d), fixed2),
                  pl.BlockSpec((1, d), fixed2)],
        out_specs=[pl.BlockSpec((tm, d), row), pl.BlockSpec((tm, 1, d), lambda i: (i, 0, 0))],
        out_shape=[jax.ShapeDtypeStruct((n, d), F32), jax.ShapeDtypeStruct((n, 1, d), F32)],
        scratch_shapes=[pltpu.VMEM((tm + 2 * POOL_HALO, d), F32), pltpu.VMEM((tm, d), BF16)],
        compiler_params=_params(("arbitrary",)),
        name="mix",
    )(attn, pgg, pgg, pgg, pgg, pgg, x2, gt_a, sc_f, sh_f, w_pool_bf, pool_scale.reshape(1, d), w_out_bf,
      g_post.reshape(1, d), g_pre_ffn.reshape(1, d))


def _router_kernel(h_ref, wrt_ref, bias_ref, tri_ref, idx_ref, rank_ref, wts_ref, cnt_ref, carry_scr, h2d_scr):
    t = h_ref.shape[0]
    neg = -jnp.inf

    @pl.when(pl.program_id(0) == 0)
    def _():
        carry_scr[...] = jnp.zeros_like(carry_scr)

    h2d_scr[...] = h_ref[...].reshape(h2d_scr.shape)
    logits = lax.dot_general(wrt_ref[...], h2d_scr[...], (((1,), (1,)), ((), ())),
                             precision=lax.Precision.HIGHEST, preferred_element_type=F32)
    scores = jax.nn.sigmoid(logits)
    choice = scores + bias_ref[...]
    iota8 = lax.broadcasted_iota(jnp.int32, (SUBLANES, t), 0)
    iota_e = lax.broadcasted_iota(jnp.int32, (N_EXPERTS, t), 0)

    gs = jnp.full((N_EXPERT_GROUPS, t), neg, F32)
    for g in range(N_EXPERT_GROUPS):
        blk = choice[g * EXPERTS_PER_GROUP:(g + 1) * EXPERTS_PER_GROUP]
        m1 = jnp.max(blk, axis=0, keepdims=True)
        first = jnp.min(jnp.where(blk == m1, iota8, EXPERTS_PER_GROUP), axis=0, keepdims=True)
        m2 = jnp.max(jnp.where(iota8 == first, neg, blk), axis=0, keepdims=True)
        gs = jnp.where(iota8 == g, m1 + m2, gs)

    gmask = jnp.zeros((N_EXPERT_GROUPS, t), F32)
    for _ in range(TOPK_GROUPS):
        m = jnp.max(gs, axis=0, keepdims=True)
        first = jnp.min(jnp.where(gs == m, iota8, N_EXPERT_GROUPS), axis=0, keepdims=True)
        hit = iota8 == first
        gmask = jnp.where(hit, 1.0, gmask)
        gs = jnp.where(hit, neg, gs)

    masked = jnp.concatenate(
        [jnp.where(jnp.max(jnp.where(iota8 == g, gmask, 0.0), axis=0, keepdims=True) > 0.0,
                   choice[g * EXPERTS_PER_GROUP:(g + 1) * EXPERTS_PER_GROUP], neg)
         for g in range(N_EXPERT_GROUPS)], axis=0)

    selmask = jnp.zeros((N_EXPERTS, t), F32)
    idx_rows, sel_rows = [], []
    for _ in range(TOP_K):
        m = jnp.max(masked, axis=0, keepdims=True)
        first = jnp.min(jnp.where(masked == m, iota_e, N_EXPERTS), axis=0, keepdims=True)
        hit = iota_e == first
        idx_rows.append(first)
        sel_rows.append(jnp.sum(jnp.where(hit, scores, 0.0), axis=0, keepdims=True))
        selmask = jnp.where(hit, 1.0, selmask)
        masked = jnp.where(hit, neg, masked)
    denom = sel_rows[0]
    for r in sel_rows[1:]:
        denom = denom + r

    prefix = jnp.dot(selmask.astype(BF16), tri_ref[...], preferred_element_type=F32)
    rankfull = prefix + carry_scr[...]
    carry = carry_scr[...] + jnp.sum(selmask, axis=1, keepdims=True)
    carry_scr[...] = carry
    cnt_ref[...] = jnp.broadcast_to(carry, cnt_ref.shape).astype(jnp.int32)

    idx_out = jnp.zeros((SUBLANES, t), jnp.int32)
    rank_out = jnp.zeros((SUBLANES, t), jnp.int32)
    wts_out = jnp.zeros((SUBLANES, t), F32)
    for k in range(TOP_K):
        rk = jnp.sum(jnp.where(iota_e == idx_rows[k], rankfull, 0.0), axis=0, keepdims=True)
        idx_out = jnp.where(iota8 == k, idx_rows[k], idx_out)
        rank_out = jnp.where(iota8 == k, rk.astype(jnp.int32), rank_out)
        wts_out = jnp.where(iota8 == k, sel_rows[k] / denom * ROUTED_SCALE, wts_out)
    idx_ref[0] = idx_out
    rank_ref[0] = rank_out
    wts_ref[0] = wts_out


def _router(h2, w_router, router_bias):
    n, _, d = h2.shape
    t = MOE_BLOCK
    nt = n // t
    tri = (jnp.arange(t)[:, None] < jnp.arange(t)[None, :]).astype(BF16)
    blk3 = pl.BlockSpec((1, SUBLANES, t), lambda i: (i, 0, 0))
    return pl.pallas_call(
        _router_kernel,
        grid=(nt,),
        in_specs=[pl.BlockSpec((t, 1, d), lambda i: (i, 0, 0)),
                  pl.BlockSpec((N_EXPERTS, d), lambda i: (0, 0)),
                  pl.BlockSpec((N_EXPERTS, 1), lambda i: (0, 0)),
                  pl.BlockSpec((t, t), lambda i: (0, 0))],
        out_specs=[blk3, blk3, blk3, pl.BlockSpec((N_EXPERTS, LANES), lambda i: (0, 0))],
        out_shape=[jax.ShapeDtypeStruct((nt, SUBLANES, t), jnp.int32),
                   jax.ShapeDtypeStruct((nt, SUBLANES, t), jnp.int32),
                   jax.ShapeDtypeStruct((nt, SUBLANES, t), F32),
                   jax.ShapeDtypeStruct((N_EXPERTS, LANES), jnp.int32)],
        scratch_shapes=[pltpu.VMEM((N_EXPERTS, 1), F32), pltpu.VMEM((t, d), F32)],
        compiler_params=_params(("arbitrary",)),
        name="router",
    )(h2, w_router.T, router_bias.reshape(N_EXPERTS, 1), tri)


ROW_DMA_UNROLL = 4


def _row_copy(src_ref, src_row, dst_ref, dst_row, sem):
    return pltpu.make_async_copy(src_ref.at[pl.ds(src_row, 1)], dst_ref.at[pl.ds(dst_row, 1)], sem)


def _dispatch_kernel(padrow_ref, dest_ref, h_ref, xs_ref, zero_scr, sem):
    t = h_ref.shape[0]

    @pl.when(pl.program_id(0) == 0)
    def _():
        zero_scr[...] = jnp.zeros_like(zero_scr)
        n_cand = padrow_ref.shape[0]

        def zero_copy(e):
            return pltpu.make_async_copy(zero_scr, xs_ref.at[pl.ds(jnp.maximum(padrow_ref[e], 0), MOE_BLOCK)], sem)

        def start(e, c):
            @pl.when(padrow_ref[e] >= 0)
            def _():
                zero_copy(e).start()
            return c

        def wait(e, c):
            @pl.when(padrow_ref[e] >= 0)
            def _():
                zero_copy(e).wait()
            return c

        lax.fori_loop(0, n_cand, start, 0)
        lax.fori_loop(0, n_cand, wait, 0)

    def start(tok, c):
        for k in range(TOP_K):
            _row_copy(h_ref, tok, xs_ref, dest_ref[0, k, tok], sem).start()
        return c

    def wait(tok, c):
        for k in range(TOP_K):
            _row_copy(h_ref, 0, xs_ref, 0, sem).wait()
        return c

    lax.fori_loop(0, t, start, 0, unroll=ROW_DMA_UNROLL)
    lax.fori_loop(0, t, wait, 0, unroll=ROW_DMA_UNROLL)


def _dispatch(h2, dest3, padrow, n_rows):
    n, _, d = h2.shape
    t = MOE_BLOCK
    smem3 = pl.BlockSpec((1, SUBLANES, t), lambda i, *_: (i, 0, 0), memory_space=pltpu.SMEM)
    return pl.pallas_call(
        _dispatch_kernel,
        grid_spec=pltpu.PrefetchScalarGridSpec(
            num_scalar_prefetch=1,
            grid=(n // t,),
            in_specs=[smem3, pl.BlockSpec((t, 1, d), lambda i, *_: (i, 0, 0))],
            out_specs=pl.BlockSpec(memory_space=pl.ANY),
            scratch_shapes=[pltpu.VMEM((MOE_BLOCK, 1, d), F32), pltpu.SemaphoreType.DMA(())]),
        out_shape=jax.ShapeDtypeStruct((n_rows, 1, d), F32),
        compiler_params=_params(("arbitrary",)),
        name="dispatch",
    )(padrow, dest3, h2)


def _expert_kernel(blk_e_ref, nused_ref, x_ref, wg_ref, wu_ref, wd_ref, y_ref, x2d_scr):
    del blk_e_ref

    @pl.when(pl.program_id(0) < nused_ref[0])
    def _():
        x2d_scr[...] = x_ref[...].reshape(x2d_scr.shape)
        x = x2d_scr[...].astype(BF16)
        g = jnp.dot(x, wg_ref[0], preferred_element_type=F32)
        u = jnp.dot(x, wu_ref[0], preferred_element_type=F32)
        h = (g * jax.nn.sigmoid(g)) * u
        y = jnp.dot(h.astype(BF16), wd_ref[0], preferred_element_type=F32)
        y_ref[...] = y.reshape(y_ref.shape)

    @pl.when(pl.program_id(0) >= nused_ref[0])
    def _():
        y_ref[...] = jnp.zeros_like(y_ref)


def _experts(xs, wg, wu, wd, blk_e, nused):
    n_rows, _, d = xs.shape
    ff = wg.shape[2]
    nb = n_rows // MOE_BLOCK
    rows = lambda b, be, nu: (jnp.minimum(b, nu[0] - 1), 0, 0)
    out_rows = lambda b, be, nu: (b, 0, 0)
    return pl.pallas_call(
        _expert_kernel,
        grid_spec=pltpu.PrefetchScalarGridSpec(
            num_scalar_prefetch=2,
            grid=(nb,),
            in_specs=[pl.BlockSpec((MOE_BLOCK, 1, d), rows),
                      pl.BlockSpec((1, d, ff), lambda b, be, nu: (be[b], 0, 0)),
                      pl.BlockSpec((1, d, ff), lambda b, be, nu: (be[b], 0, 0)),
                      pl.BlockSpec((1, ff, d), lambda b, be, nu: (be[b], 0, 0))],
            out_specs=pl.BlockSpec((MOE_BLOCK, 1, d), out_rows),
            scratch_shapes=[pltpu.VMEM((MOE_BLOCK, d), F32)]),
        out_shape=jax.ShapeDtypeStruct((n_rows, 1, d), F32),
        compiler_params=_params(("arbitrary",)),
        name="experts",
    )(blk_e, nused, xs, wg, wu, wd)


NCH_GU = 4
NCH_D = 4


def _experts_streamed_kernel(nused_ref, blk_e_ref, first_ref, slot_ref, nxt_ref, lo_gu_ref, hi_gu_ref, lo_d_ref,
                             hi_d_ref, x_ref, wg_hbm, wu_hbm, wd_hbm, y_ref,
                             wg_bf, wu_bf, wd_bf, stg_g, stg_u, stg_d, x2d_scr, sems):
    b = pl.program_id(0)
    d, ff = wg_bf.shape[1], wg_bf.shape[2]
    ch_gu, ch_d = d // NCH_GU, ff // NCH_D
    mats = ((wg_hbm, stg_g, wg_bf, ch_gu, NCH_GU), (wu_hbm, stg_u, wu_bf, ch_gu, NCH_GU),
            (wd_hbm, stg_d, wd_bf, ch_d, NCH_D))

    def piece_copy(m, e, c):
        src, stg, _, ch, _ = mats[m]
        return pltpu.make_async_copy(src.at[e, pl.ds(pl.multiple_of(c * ch, ch), ch), :], stg, sems.at[m])

    def stream(m, e, slot, lo, hi):
        _, stg, dst, ch, nch = mats[m]

        def step(c, carry):
            piece_copy(m, e, c).wait()
            dst[slot, pl.ds(pl.multiple_of(c * ch, ch), ch), :] = stg[...].astype(BF16)

            @pl.when(c + 1 < nch)
            def _():
                piece_copy(m, e, c + 1).start()

            return carry

        lax.fori_loop(lo, hi, step, 0)

    @pl.when(b < nused_ref[0])
    def _():
        e_next = nxt_ref[b]
        slot = slot_ref[b]
        other = 1 - slot

        @pl.when(b == 0)
        def _():
            for m in range(3):
                piece_copy(m, blk_e_ref[0], 0).start()
            for m in range(3):
                stream(m, blk_e_ref[0], slot, 0, mats[m][4])

        @pl.when(jnp.logical_and(first_ref[b] == 1, e_next >= 0))
        def _():
            for m in range(3):
                piece_copy(m, e_next, 0).start()

        def prefetch(m, lo_ref, hi_ref):
            @pl.when(e_next >= 0)
            def _():
                stream(m, e_next, other, lo_ref[b], hi_ref[b])

        x2d_scr[...] = x_ref[...].reshape(x2d_scr.shape)
        x = x2d_scr[...].astype(BF16)
        g = jnp.dot(x, wg_bf[slot], preferred_element_type=F32)
        prefetch(0, lo_gu_ref, hi_gu_ref)
        u = jnp.dot(x, wu_bf[slot], preferred_element_type=F32)
        prefetch(1, lo_gu_ref, hi_gu_ref)
        h = (g * jax.nn.sigmoid(g)) * u
        y = jnp.dot(h.astype(BF16), wd_bf[slot], preferred_element_type=F32)
        prefetch(2, lo_d_ref, hi_d_ref)
        y_ref[...] = y.reshape(y_ref.shape)

    @pl.when(b >= nused_ref[0])
    def _():
        y_ref[...] = jnp.zeros_like(y_ref)


def _stream_schedule(padded, nused, n_blocks):
    i32 = jnp.int32
    nblk = (padded // MOE_BLOCK).astype(i32)
    bend = jnp.cumsum(nblk)
    bstart = bend - nblk
    blk = jnp.minimum(jnp.arange(n_blocks, dtype=i32), nused - 1)
    blk_e = jnp.minimum(jnp.sum(bend[None, :] <= blk[:, None], axis=1), N_EXPERTS - 1).astype(i32)
    r = blk - bstart[blk_e]
    nr = jnp.maximum(nblk[blk_e], 1)
    e_ids = jnp.arange(N_EXPERTS, dtype=i32)
    live = jnp.where(nblk > 0, e_ids, N_EXPERTS)
    suffix = lax.cummin(live[::-1])[::-1]
    nxt = jnp.concatenate([suffix[1:], jnp.full((1,), N_EXPERTS, i32)])
    nxt = jnp.where(nxt >= N_EXPERTS, -1, nxt)
    ordinal = jnp.cumsum((nblk > 0).astype(i32)) - 1
    first = (r == 0).astype(i32)
    slot = (ordinal[blk_e] % 2).astype(i32)
    pieces = lambda nch: ((nch * r) // nr, (nch * (r + 1)) // nr)
    lo_gu, hi_gu = pieces(NCH_GU)
    lo_d, hi_d = pieces(NCH_D)
    return blk_e, first, slot, nxt[blk_e], lo_gu.astype(i32), hi_gu.astype(i32), lo_d.astype(i32), hi_d.astype(i32)


def _experts_streamed(xs, wg, wu, wd, padded, nused):
    n_rows, _, d = xs.shape
    ff = wg.shape[2]
    nb = n_rows // MOE_BLOCK
    sched = _stream_schedule(padded, nused[0], nb)
    rows = lambda b, nu, *_: (jnp.minimum(b, nu[0] - 1), 0, 0)
    hbm = pl.BlockSpec(memory_space=pl.ANY)
    return pl.pallas_call(
        _experts_streamed_kernel,
        grid_spec=pltpu.PrefetchScalarGridSpec(
            num_scalar_prefetch=9,
            grid=(nb,),
            in_specs=[pl.BlockSpec((MOE_BLOCK, 1, d), rows), hbm, hbm, hbm],
            out_specs=pl.BlockSpec((MOE_BLOCK, 1, d), lambda b, *_: (b, 0, 0)),
            scratch_shapes=[pltpu.VMEM((2, d, ff), BF16), pltpu.VMEM((2, d, ff), BF16), pltpu.VMEM((2, ff, d), BF16),
                            pltpu.VMEM((d // NCH_GU, ff), F32), pltpu.VMEM((d // NCH_GU, ff), F32),
                            pltpu.VMEM((ff // NCH_D, d), F32),
                            pltpu.VMEM((MOE_BLOCK, d), F32),
                            pltpu.SemaphoreType.DMA((3,))]),
        out_shape=jax.ShapeDtypeStruct((n_rows, 1, d), F32),
        compiler_params=_params(("arbitrary",)),
        name="experts_routed",
    )(nused, *sched, xs, wg, wu, wd)


def _combine_kernel(dest_ref, wts_ref, ysh_ref, x1_ref, gt_ref, gpost_ref, yr_ref,
                    out_ref, buf, row2d_scr, sem):
    t = x1_ref.shape[0]

    def copy(tok, k):
        return _row_copy(yr_ref, dest_ref[0, k, tok], buf.at[k], tok, sem)

    def start(tok, c):
        for k in range(TOP_K):
            copy(tok, k).start()
        return c

    def wait(tok, c):
        for k in range(TOP_K):
            _row_copy(yr_ref, 0, buf.at[k], 0, sem).wait()
        return c

    lax.fori_loop(0, t, start, 0, unroll=ROW_DMA_UNROLL)
    lax.fori_loop(0, t, wait, 0, unroll=ROW_DMA_UNROLL)
    row2d_scr[...] = ysh_ref[...].reshape(row2d_scr.shape)
    y = row2d_scr[...]
    for k in range(TOP_K):
        row2d_scr[...] = buf[k].reshape(row2d_scr.shape)
        y = y + wts_ref[:, k:k + 1] * row2d_scr[...]
    out_ref[...] = x1_ref[...] + gt_ref[0] * (_rms(y) * gpost_ref[...])


def _combine(yr, dest3, wts, ysh, x1, gt_f, g_post, seq_len):
    n, d = x1.shape
    t = MOE_BLOCK
    tiles_per_seq = seq_len // t
    smem3 = pl.BlockSpec((1, SUBLANES, t), lambda i, *_: (i, 0, 0), memory_space=pltpu.SMEM)
    row = lambda i, *_: (i, 0)
    return pl.pallas_call(
        _combine_kernel,
        grid_spec=pltpu.PrefetchScalarGridSpec(
            num_scalar_prefetch=0,
            grid=(n // t,),
            in_specs=[smem3,
                      pl.BlockSpec((t, SUBLANES), row),
                      pl.BlockSpec((t, 1, d), lambda i, *_: (i, 0, 0)),
                      pl.BlockSpec((t, d), row),
                      pl.BlockSpec((1, 1, d), lambda i, *_: (i // tiles_per_seq, 0, 0)),
                      pl.BlockSpec((1, d), lambda i, *_: (0, 0)),
                      pl.BlockSpec(memory_space=pl.ANY)],
            out_specs=pl.BlockSpec((t, d), row),
            scratch_shapes=[pltpu.VMEM((TOP_K, t, 1, d), F32), pltpu.VMEM((t, d), F32),
                            pltpu.SemaphoreType.DMA(())]),
        out_shape=jax.ShapeDtypeStruct((n, d), F32),
        compiler_params=_params(("arbitrary",)),
        name="combine",
    )(dest3, wts, ysh, x1, gt_f, g_post.reshape(1, d), yr)


def _layer(x2, c, w_mod, b_mod, g_pre_mix, g_post_mix, g_pre_ffn, g_post_ffn, w_in, q_norm_g, k_norm_g,
           w_pool, pool_scale, w_out, w_router, router_bias, w_exp_gate, w_exp_up, w_exp_down,
           w_sh_gate, w_sh_up, w_sh_down, bsz, seq_len):
    n, d = x2.shape
    mod = _mod(c, w_mod, b_mod).reshape(bsz, N_MOD, 1, d)
    sh_a, sc_a, gt_a, sh_f, sc_f, gt_f = (mod[:, i] for i in range(N_MOD))

    qkv, pgg = _inproj(x2, g_pre_mix, sc_a, sh_a, w_in.astype(BF16), q_norm_g, k_norm_g, seq_len)
    attn = _attention(qkv, q_norm_g, k_norm_g, bsz, seq_len, d)
    x1, h2 = _mix(attn, pgg, x2, gt_a, sc_f, sh_f, w_pool.astype(BF16), pool_scale, w_out.astype(BF16),
                  g_post_mix, g_pre_ffn, seq_len)

    idx3, rank3, wts3, cnt = _router(h2, w_router, router_bias)
    counts = cnt[:, 0]
    padded = ((counts + MOE_BLOCK - 1) // MOE_BLOCK) * MOE_BLOCK
    pend = jnp.cumsum(padded)
    pstart = (pend - padded).astype(jnp.int32)
    n_rows = n * TOP_K + N_EXPERTS * MOE_BLOCK
    n_blocks = n_rows // MOE_BLOCK
    nused = (pend[-1] // MOE_BLOCK).astype(jnp.int32).reshape(1)
    all_blocks = jnp.arange(n_blocks, dtype=jnp.int32)
    padrow = jnp.concatenate([jnp.where(padded > 0, pend.astype(jnp.int32) - MOE_BLOCK, -1),
                              jnp.where(all_blocks >= nused[0], all_blocks * MOE_BLOCK, -1)])

    dest3 = rank3 + jnp.sum(jnp.where(idx3[..., None] == jnp.arange(N_EXPERTS, dtype=jnp.int32), pstart, 0), axis=-1)
    xs = _dispatch(h2, dest3, padrow, n_rows)
    yr = _experts_streamed(xs, w_exp_gate, w_exp_up, w_exp_down, padded, nused)
    ysh = _experts(h2, w_sh_gate.astype(BF16)[None], w_sh_up.astype(BF16)[None], w_sh_down.astype(BF16)[None],
                   jnp.zeros((n // MOE_BLOCK,), jnp.int32), jnp.full((1,), n // MOE_BLOCK, jnp.int32))
    wts = wts3.transpose(0, 2, 1).reshape(n, SUBLANES)
    return _combine(yr, dest3, wts, ysh, x1, gt_f, g_post_ffn, seq_len)


def kernel(x, c, w_mod, b_mod, g_pre_mix, g_post_mix, g_pre_ffn, g_post_ffn, w_in, q_norm_g, k_norm_g, w_pool,
           pool_scale, w_out, w_router, router_bias, w_exp_gate, w_exp_up, w_exp_down, w_sh_gate, w_sh_up,
           w_sh_down):
    bsz, seq_len, d = x.shape
    x2 = x.reshape(bsz * seq_len, d)
    for l in range(w_mod.shape[0]):
        x2 = _layer(x2, c, w_mod[l], b_mod[l], g_pre_mix[l], g_post_mix[l], g_pre_ffn[l], g_post_ffn[l],
                    w_in[l], q_norm_g[l], k_norm_g[l], w_pool[l], pool_scale[l], w_out[l], w_router[l],
                    router_bias[l], w_exp_gate[l], w_exp_up[l], w_exp_down[l], w_sh_gate[l], w_sh_up[l],
                    w_sh_down[l], bsz, seq_len)
    return x2.reshape(bsz, seq_len, d)
```

```python
import functools
import math

import jax
import jax.numpy as jnp
import numpy as np
from jax import lax
from jax.experimental import pallas as pl
from jax.experimental.pallas import tpu as pltpu

F32 = jnp.float32
BF16 = jnp.bfloat16

NORM_EPS = 1e-6
HEAD_DIM = 128
KV_GROUP = 4
GRID_W = 64
ROPE_THETA = 10000.0
ROPE_AXIS_DIM = HEAD_DIM // 2
POOL_WINDOWS = (2, 4, 8, 16)
POOL_HALO = 8
N_EXPERTS = 64
TOP_K = 6
N_EXPERT_GROUPS = 8
TOPK_GROUPS = 4
EXPERTS_PER_GROUP = N_EXPERTS // N_EXPERT_GROUPS
ROUTED_SCALE = 2.5
MOE_BLOCK = 256
N_MOD = 6
LANES = 128
SUBLANES = 8
VMEM_LIMIT = 56 * 1024 * 1024

Q_PRESCALE = (HEAD_DIM ** -0.5) * math.log2(math.e)


def _rms(x):
    return x * lax.rsqrt(jnp.mean(x * x, axis=-1, keepdims=True) + NORM_EPS)


def _params(sem, vmem=VMEM_LIMIT):
    return pltpu.CompilerParams(dimension_semantics=sem, vmem_limit_bytes=vmem)


def _mod_kernel(ct_ref, w_ref, b_ref, o_ref):
    ct = ct_ref[...]
    s = ct * jax.nn.sigmoid(ct)
    w = w_ref[...]
    rows = [jnp.sum(w * s[:, b:b + 1], axis=0, keepdims=True) for b in range(ct.shape[1])]
    o_ref[...] = jnp.concatenate(rows, axis=0) + b_ref[...]


def _mod(c, w_mod, b_mod):
    bsz, d = c.shape
    n = w_mod.shape[1]
    tn = min(512, n)
    return pl.pallas_call(
        _mod_kernel,
        grid=(n // tn,),
        in_specs=[pl.BlockSpec((d, bsz), lambda j: (0, 0)),
                  pl.BlockSpec((d, tn), lambda j: (0, j)),
                  pl.BlockSpec((1, tn), lambda j: (0, j))],
        out_specs=pl.BlockSpec((bsz, tn), lambda j: (0, j)),
        out_shape=jax.ShapeDtypeStruct((bsz, n), F32),
        compiler_params=_params(("arbitrary",)),
        name="mod",
    )(c.T, w_mod, b_mod.reshape(1, n))


def _rope_tables(seq_len):
    t = np.arange(seq_len)
    row = (t // GRID_W).astype(np.float32)
    col = (t % GRID_W).astype(np.float32)
    inv_freq = (ROPE_THETA ** (-np.arange(0, ROPE_AXIS_DIM, 2, dtype=np.float32) / ROPE_AXIS_DIM)).astype(np.float32)
    ang_r = row[:, None] * inv_freq[None, :]
    ang_c = col[:, None] * inv_freq[None, :]
    cr, sr, cc, sc = np.cos(ang_r), np.sin(ang_r), np.cos(ang_c), np.sin(ang_c)
    z = np.zeros_like(sr)
    cos = np.concatenate([cr, cr, cc, cc], axis=1)
    sa = np.concatenate([-sr, z, -sc, z], axis=1)
    sb = np.concatenate([z, sr, z, sc], axis=1)
    return tuple(jnp.asarray(a, F32) for a in (cos, sa, sb))


def _inproj_kernel(x_ref, g_ref, sc_ref, sh_ref, w_ref, gq_ref, gk_ref, cos_ref, sa_ref, sb_ref,
                   qkv_ref, f32_ref, h_scr, *, nq):
    j = pl.program_id(1)

    @pl.when(j == 0)
    def _():
        h = _rms(x_ref[...]) * g_ref[...]
        h = h * (1.0 + sc_ref[0]) + sh_ref[0]
        h_scr[...] = h.astype(BF16)

    acc = jnp.dot(h_scr[...], w_ref[...], preferred_element_type=F32)
    heads = acc.shape[1] // HEAD_DIM

    def qk_epilogue(gain):
        cos, sa, sb = cos_ref[...], sa_ref[...], sb_ref[...]
        for hh in range(heads):
            sl = slice(hh * HEAD_DIM, (hh + 1) * HEAD_DIM)
            y = _rms(acc[:, sl]) * gain
            r = y * cos + pltpu.roll(y, 3 * HEAD_DIM // 4, 1) * sa + pltpu.roll(y, HEAD_DIM // 4, 1) * sb
            qkv_ref[:, sl] = r.astype(BF16)

    @pl.when(j < nq)
    def _():
        qk_epilogue(gq_ref[...] * Q_PRESCALE)

    @pl.when(j == nq)
    def _():
        qk_epilogue(gk_ref[...])

    @pl.when(j == nq + 1)
    def _():
        qkv_ref[...] = acc.astype(BF16)

    @pl.when(j > nq + 1)
    def _():
        f32_ref[...] = acc


def _inproj(x2, g_pre, sc, sh, w_in_bf, gq, gk, seq_len):
    n, d = x2.shape
    kv_w = d // KV_GROUP
    tn = kv_w
    nq = d // tn
    n_qkv = nq + 2
    n_f32 = 3 * d // tn
    tm = min(1024, seq_len)
    tiles_per_seq = seq_len // tm
    cos, sa, sb = _rope_tables(seq_len)
    row = lambda i, j: (i, 0)
    bat = lambda i, j: (i // tiles_per_seq, 0, 0)
    pos = lambda i, j: (i % tiles_per_seq, 0)
    fixed = lambda i, j: (0, 0)
    return pl.pallas_call(
        functools.partial(_inproj_kernel, nq=nq),
        grid=(n // tm, n_qkv + n_f32),
        in_specs=[pl.BlockSpec((tm, d), row),
                  pl.BlockSpec((1, d), fixed),
                  pl.BlockSpec((1, 1, d), bat),
                  pl.BlockSpec((1, 1, d), bat),
                  pl.BlockSpec((d, tn), lambda i, j: (0, j)),
                  pl.BlockSpec((1, HEAD_DIM), fixed),
                  pl.BlockSpec((1, HEAD_DIM), fixed),
                  pl.BlockSpec((tm, HEAD_DIM), pos),
                  pl.BlockSpec((tm, HEAD_DIM), pos),
                  pl.BlockSpec((tm, HEAD_DIM), pos)],
        out_specs=[pl.BlockSpec((tm, tn), lambda i, j: (i, jnp.minimum(j, n_qkv - 1))),
                   pl.BlockSpec((tm, tn), lambda i, j: (i, jnp.maximum(j - n_qkv, 0)))],
        out_shape=[jax.ShapeDtypeStruct((n, n_qkv * tn), BF16),
                   jax.ShapeDtypeStruct((n, n_f32 * tn), F32)],
        scratch_shapes=[pltpu.VMEM((tm, d), BF16)],
        compiler_params=_params(("arbitrary", "arbitrary")),
        name="inproj",
    )(x2, g_pre.reshape(1, d), sc, sh, w_in_bf, gq.reshape(1, HEAD_DIM), gk.reshape(1, HEAD_DIM),
      cos, sa, sb)


def _attn_kernel(q_ref, k_ref, v_ref, o_ref, *, tk):
    tq = q_ref.shape[0]
    seq_len = k_ref.shape[0]
    qs = jnp.concatenate([q_ref[:, g * HEAD_DIM:(g + 1) * HEAD_DIM] for g in range(KV_GROUP)], axis=0)
    rows = qs.shape[0]

    def body(c, carry):
        m, l, acc = carry
        start = pl.multiple_of(c * tk, tk)
        kc = k_ref[pl.ds(start, tk), :]
        vc = v_ref[pl.ds(start, tk), :]
        s = lax.dot_general(qs, kc, (((1,), (1,)), ((), ())), preferred_element_type=F32)
        m_new = jnp.maximum(m, jnp.max(s, axis=-1, keepdims=True))
        alpha = jnp.exp2(m - m_new)
        p = jnp.exp2(s - m_new)
        l = alpha * l + jnp.sum(p, axis=-1, keepdims=True)
        acc = alpha * acc + jnp.dot(p.astype(BF16), vc, preferred_element_type=F32)
        return m_new, l, acc

    init = (jnp.full((rows, 1), -jnp.inf, F32), jnp.zeros((rows, 1), F32), jnp.zeros((rows, HEAD_DIM), F32))
    _, l, acc = lax.fori_loop(0, seq_len // tk, body, init)
    o = acc / l
    for g in range(KV_GROUP):
        o_ref[:, g * HEAD_DIM:(g + 1) * HEAD_DIM] = o[g * tq:(g + 1) * tq].astype(BF16)


ATTN_SCORE_BUFS = 2


def _attn_bounded_kernel(q_ref, k_ref, v_ref, o_ref, *s_scr, tk):
    tq = q_ref.shape[0]
    seq_len = k_ref.shape[0]
    qs = jnp.concatenate([q_ref[:, g * HEAD_DIM:(g + 1) * HEAD_DIM] for g in range(KV_GROUP)], axis=0)
    rows = qs.shape[0]
    nchunks = seq_len // tk

    def qk(c, dst):
        kc = k_ref[c * tk:(c + 1) * tk, :]
        dst[...] = lax.dot_general(kc, qs, (((1,), (1,)), ((), ())), preferred_element_type=F32)

    def pv(c, src, l, acc):
        vc = v_ref[c * tk:(c + 1) * tk, :]
        pt = jnp.exp2(src[...])
        l = l + jnp.sum(pt.reshape(tk // SUBLANES, SUBLANES, rows), axis=0)
        acc = acc + lax.dot_general(vc, pt.astype(BF16), (((0,), (0,)), ((), ())),
                                    preferred_element_type=F32)
        return l, acc

    l, acc = jnp.zeros((SUBLANES, rows), F32), jnp.zeros((HEAD_DIM, rows), F32)
    qk(0, s_scr[0])
    for c in range(nchunks):
        if c + 1 < nchunks:
            qk(c + 1, s_scr[(c + 1) % ATTN_SCORE_BUFS])
        l, acc = pv(c, s_scr[c % ATTN_SCORE_BUFS], l, acc)
    o = (acc / jnp.sum(l, axis=0, keepdims=True)).T
    for g in range(KV_GROUP):
        o_ref[:, g * HEAD_DIM:(g + 1) * HEAD_DIM] = o[g * tq:(g + 1) * tq].astype(BF16)


ATTN_SCORE_BOUND = 60.0


def _attention(qkv, gq, gk, bsz, seq_len, d):
    n = bsz * seq_len
    n_kv = d // HEAD_DIM // KV_GROUP
    gw = KV_GROUP * HEAD_DIM
    tq = min(128, seq_len)
    tk = min(512, seq_len)
    qt = seq_len // tq
    k_col0 = d // HEAD_DIM
    v_col0 = k_col0 + n_kv

    def call(body, scratch):
        return pl.pallas_call(
            functools.partial(body, tk=tk),
            grid=(bsz, n_kv, qt),
            in_specs=[pl.BlockSpec((tq, gw), lambda b, h, i: (b * qt + i, h)),
                      pl.BlockSpec((seq_len, HEAD_DIM), lambda b, h, i: (b, k_col0 + h)),
                      pl.BlockSpec((seq_len, HEAD_DIM), lambda b, h, i: (b, v_col0 + h))],
            out_specs=pl.BlockSpec((tq, gw), lambda b, h, i: (b * qt + i, h)),
            out_shape=jax.ShapeDtypeStruct((n, d), BF16),
            scratch_shapes=scratch,
            compiler_params=_params(("arbitrary", "arbitrary", "arbitrary")),
            name="attn",
        )(qkv, qkv, qkv)

    online = lambda: call(_attn_kernel, [])
    bound = 1.02 * HEAD_DIM * Q_PRESCALE * jnp.max(jnp.abs(gq)) * jnp.max(jnp.abs(gk))
    bounded = lambda: call(_attn_bounded_kernel, [pltpu.VMEM((tk, KV_GROUP * tq), F32)] * ATTN_SCORE_BUFS)
    return lax.cond(bound <= ATTN_SCORE_BOUND, bounded, online)


def _mix_kernel(attn_ref, p_ref, pprev_ref, pnext_ref, ga_ref, gp_ref, x_ref, gt_ref, scf_ref, shf_ref,
                wpool_ref, pscale_ref, wout_ref, gpost_ref, gpre_ref,
                x1_ref, h2_ref, pext_scr, merged_scr, *, seq_len):
    tm, d = p_ref.shape
    gw = d // len(POOL_WINDOWS)
    tiles_per_seq = seq_len // tm
    ti = pl.program_id(0) % tiles_per_seq
    pext_scr[0:POOL_HALO, :] = jnp.where(ti > 0, pprev_ref[...], 0.0)
    pext_scr[POOL_HALO:POOL_HALO + tm, :] = p_ref[...]
    pext_scr[POOL_HALO + tm:2 * POOL_HALO + tm, :] = jnp.where(ti < tiles_per_seq - 1, pnext_ref[...], 0.0)
    tpos = ti * tm + lax.broadcasted_iota(jnp.int32, (tm, 1), 0)
    for gi, w in enumerate(POOL_WINDOWS):
        cs = slice(gi * gw, (gi + 1) * gw)
        win = pext_scr[POOL_HALO - w // 2:POOL_HALO - w // 2 + tm, cs]
        for o in range(1 - w // 2, w // 2):
            win = win + pext_scr[POOL_HALO + o:POOL_HALO + o + tm, cs]
        cnt = (jnp.minimum(tpos + w // 2, seq_len) - jnp.maximum(tpos - w // 2, 0)).astype(F32)
        pooled = win / cnt - p_ref[:, cs]
        mixed = jnp.dot(pooled.astype(BF16), wpool_ref[gi], preferred_element_type=F32) * pscale_ref[:, cs]
        merged = (jax.nn.sigmoid(ga_ref[:, cs]) * attn_ref[:, cs].astype(F32)
                  + jax.nn.sigmoid(gp_ref[:, cs]) * mixed)
        merged_scr[:, cs] = merged.astype(BF16)
    y = jnp.dot(merged_scr[...], wout_ref[...], preferred_element_type=F32)
    x1 = x_ref[...] + gt_ref[0] * (_rms(y) * gpost_ref[...])
    x1_ref[...] = x1
    h2 = (_rms(x1) * gpre_ref[...]) * (1.0 + scf_ref[0]) + shf_ref[0]
    h2_ref[...] = h2.reshape(h2_ref.shape)


def _mix(attn, pgg, x2, gt_a, sc_f, sh_f, w_pool_bf, pool_scale, w_out_bf, g_post, g_pre_ffn, seq_len):
    n, d = x2.shape
    tm = min(256, seq_len)
    tiles_per_seq = seq_len // tm
    hb = tm // POOL_HALO
    n_hb = n // POOL_HALO
    row = lambda i: (i, 0)
    bat = lambda i: (i // tiles_per_seq, 0, 0)
    fixed2 = lambda i: (0, 0)
    ng = len(POOL_WINDOWS)
    gw = d // ng
    return pl.pallas_call(
        functools.partial(_mix_kernel, seq_len=seq_len),
        grid=(n // tm,),
        in_specs=[pl.BlockSpec((tm, d), row),
                  pl.BlockSpec((tm, d), lambda i: (i, 0)),
                  pl.BlockSpec((POOL_HALO, d), lambda i: (jnp.maximum(i * hb - 1, 0), 0)),
                  pl.BlockSpec((POOL_HALO, d), lambda i: (jnp.minimum((i + 1) * hb, n_hb - 1), 0)),
                  pl.BlockSpec((tm, d), lambda i: (i, 1)),
                  pl.BlockSpec((tm, d), lambda i: (i, 2)),
                  pl.BlockSpec((tm, d), row),
                  pl.BlockSpec((1, 1, d), bat),
                  pl.BlockSpec((1, 1, d), bat),
                  pl.BlockSpec((1, 1, d), bat),
                  pl.BlockSpec((ng, gw, gw), lambda i: (0, 0, 0)),
                  pl.BlockSpec((1, d), fixed2),
                  pl.BlockSpec((d, d), fixed2),
                  pl.BlockSpec((1, d), fixed2),
                  pl.BlockSpec((1, d), fixed2)],
        out_specs=[pl.BlockSpec((tm, d), row), pl.BlockSpec((tm, 1, d), lambda i: (i, 0, 0))],
        out_shape=[jax.ShapeDtypeStruct((n, d), F32), jax.ShapeDtypeStruct((n, 1, d), F32)],
        scratch_shapes=[pltpu.VMEM((tm + 2 * POOL_HALO, d), F32), pltpu.VMEM((tm, d), BF16)],
        compiler_params=_params(("arbitrary",)),
        name="mix",
    )(attn, pgg, pgg, pgg, pgg, pgg, x2, gt_a, sc_f, sh_f, w_pool_bf, pool_scale.reshape(1, d), w_out_bf,
      g_post.reshape(1, d), g_pre_ffn.reshape(1, d))


def _router_kernel(h_ref, wrt_ref, bias_ref, tri_ref, idx_ref, rank_ref, wts_ref, cnt_ref, carry_scr, h2d_scr):
    t = h_ref.shape[0]
    neg = -jnp.inf

    @pl.when(pl.program_id(0) == 0)
    def _():
        carry_scr[...] = jnp.zeros_like(carry_scr)

    h2d_scr[...] = h_ref[...].reshape(h2d_scr.shape)
    logits = lax.dot_general(wrt_ref[...], h2d_scr[...], (((1,), (1,)), ((), ())),
                             precision=lax.Precision.HIGHEST, preferred_element_type=F32)
    scores = jax.nn.sigmoid(logits)
    choice = scores + bias_ref[...]
    iota8 = lax.broadcasted_iota(jnp.int32, (SUBLANES, t), 0)
    iota_e = lax.broadcasted_iota(jnp.int32, (N_EXPERTS, t), 0)

    gs = jnp.full((N_EXPERT_GROUPS, t), neg, F32)
    for g in range(N_EXPERT_GROUPS):
        blk = choice[g * EXPERTS_PER_GROUP:(g + 1) * EXPERTS_PER_GROUP]
        m1 = jnp.max(blk, axis=0, keepdims=True)
        first = jnp.min(jnp.where(blk == m1, iota8, EXPERTS_PER_GROUP), axis=0, keepdims=True)
        m2 = jnp.max(jnp.where(iota8 == first, neg, blk), axis=0, keepdims=True)
        gs = jnp.where(iota8 == g, m1 + m2, gs)

    gmask = jnp.zeros((N_EXPERT_GROUPS, t), F32)
    for _ in range(TOPK_GROUPS):
        m = jnp.max(gs, axis=0, keepdims=True)
        first = jnp.min(jnp.where(gs == m, iota8, N_EXPERT_GROUPS), axis=0, keepdims=True)
        hit = iota8 == first
        gmask = jnp.where(hit, 1.0, gmask)
        gs = jnp.where(hit, neg, gs)

    masked = jnp.concatenate(
        [jnp.where(jnp.max(jnp.where(iota8 == g, gmask, 0.0), axis=0, keepdims=True) > 0.0,
                   choice[g * EXPERTS_PER_GROUP:(g + 1) * EXPERTS_PER_GROUP], neg)
         for g in range(N_EXPERT_GROUPS)], axis=0)

    selmask = jnp.zeros((N_EXPERTS, t), F32)
    idx_rows, sel_rows = [], []
    for _ in range(TOP_K):
        m = jnp.max(masked, axis=0, keepdims=True)
        first = jnp.min(jnp.where(masked == m, iota_e, N_EXPERTS), axis=0, keepdims=True)
        hit = iota_e == first
        idx_rows.append(first)
        sel_rows.append(jnp.sum(jnp.where(hit, scores, 0.0), axis=0, keepdims=True))
        selmask = jnp.where(hit, 1.0, selmask)
        masked = jnp.where(hit, neg, masked)
    denom = sel_rows[0]
    for r in sel_rows[1:]:
        denom = denom + r

    prefix = jnp.dot(selmask.astype(BF16), tri_ref[...], preferred_element_type=F32)
    rankfull = prefix + carry_scr[...]
    carry = carry_scr[...] + jnp.sum(selmask, axis=1, keepdims=True)
    carry_scr[...] = carry
    cnt_ref[...] = jnp.broadcast_to(carry, cnt_ref.shape).astype(jnp.int32)

    idx_out = jnp.zeros((SUBLANES, t), jnp.int32)
    rank_out = jnp.zeros((SUBLANES, t), jnp.int32)
    wts_out = jnp.zeros((SUBLANES, t), F32)
    for k in range(TOP_K):
        rk = jnp.sum(jnp.where(iota_e == idx_rows[k], rankfull, 0.0), axis=0, keepdims=True)
        idx_out = jnp.where(iota8 == k, idx_rows[k], idx_out)
        rank_out = jnp.where(iota8 == k, rk.astype(jnp.int32), rank_out)
        wts_out = jnp.where(iota8 == k, sel_rows[k] / denom * ROUTED_SCALE, wts_out)
    idx_ref[0] = idx_out
    rank_ref[0] = rank_out
    wts_ref[0] = wts_out


def _router(h2, w_router, router_bias):
    n, _, d = h2.shape
    t = MOE_BLOCK
    nt = n // t
    tri = (jnp.arange(t)[:, None] < jnp.arange(t)[None, :]).astype(BF16)
    blk3 = pl.BlockSpec((1, SUBLANES, t), lambda i: (i, 0, 0))
    return pl.pallas_call(
        _router_kernel,
        grid=(nt,),
        in_specs=[pl.BlockSpec((t, 1, d), lambda i: (i, 0, 0)),
                  pl.BlockSpec((N_EXPERTS, d), lambda i: (0, 0)),
                  pl.BlockSpec((N_EXPERTS, 1), lambda i: (0, 0)),
                  pl.BlockSpec((t, t), lambda i: (0, 0))],
        out_specs=[blk3, blk3, blk3, pl.BlockSpec((N_EXPERTS, LANES), lambda i: (0, 0))],
        out_shape=[jax.ShapeDtypeStruct((nt, SUBLANES, t), jnp.int32),
                   jax.ShapeDtypeStruct((nt, SUBLANES, t), jnp.int32),
                   jax.ShapeDtypeStruct((nt, SUBLANES, t), F32),
                   jax.ShapeDtypeStruct((N_EXPERTS, LANES), jnp.int32)],
        scratch_shapes=[pltpu.VMEM((N_EXPERTS, 1), F32), pltpu.VMEM((t, d), F32)],
        compiler_params=_params(("arbitrary",)),
        name="router",
    )(h2, w_router.T, router_bias.reshape(N_EXPERTS, 1), tri)


ROW_DMA_UNROLL = 4


def _row_copy(src_ref, src_row, dst_ref, dst_row, sem):
    return pltpu.make_async_copy(src_ref.at[pl.ds(src_row, 1)], dst_ref.at[pl.ds(dst_row, 1)], sem)


def _dispatch_kernel(padrow_ref, dest_ref, h_ref, xs_ref, zero_scr, sem):
    t = h_ref.shape[0]

    @pl.when(pl.program_id(0) == 0)
    def _():
        zero_scr[...] = jnp.zeros_like(zero_scr)
        n_cand = padrow_ref.shape[0]

        def zero_copy(e):
            return pltpu.make_async_copy(zero_scr, xs_ref.at[pl.ds(jnp.maximum(padrow_ref[e], 0), MOE_BLOCK)], sem)

        def start(e, c):
            @pl.when(padrow_ref[e] >= 0)
            def _():
                zero_copy(e).start()
            return c

        def wait(e, c):
            @pl.when(padrow_ref[e] >= 0)
            def _():
                zero_copy(e).wait()
            return c

        lax.fori_loop(0, n_cand, start, 0)
        lax.fori_loop(0, n_cand, wait, 0)

    def start(tok, c):
        for k in range(TOP_K):
            _row_copy(h_ref, tok, xs_ref, dest_ref[0, k, tok], sem).start()
        return c

    def wait(tok, c):
        for k in range(TOP_K):
            _row_copy(h_ref, 0, xs_ref, 0, sem).wait()
        return c

    lax.fori_loop(0, t, start, 0, unroll=ROW_DMA_UNROLL)
    lax.fori_loop(0, t, wait, 0, unroll=ROW_DMA_UNROLL)


def _dispatch(h2, dest3, padrow, n_rows):
    n, _, d = h2.shape
    t = MOE_BLOCK
    smem3 = pl.BlockSpec((1, SUBLANES, t), lambda i, *_: (i, 0, 0), memory_space=pltpu.SMEM)
    return pl.pallas_call(
        _dispatch_kernel,
        grid_spec=pltpu.PrefetchScalarGridSpec(
            num_scalar_prefetch=1,
            grid=(n // t,),
            in_specs=[smem3, pl.BlockSpec((t, 1, d), lambda i, *_: (i, 0, 0))],
            out_specs=pl.BlockSpec(memory_space=pl.ANY),
            scratch_shapes=[pltpu.VMEM((MOE_BLOCK, 1, d), F32), pltpu.SemaphoreType.DMA(())]),
        out_shape=jax.ShapeDtypeStruct((n_rows, 1, d), F32),
        compiler_params=_params(("arbitrary",)),
        name="dispatch",
    )(padrow, dest3, h2)


def _expert_kernel(blk_e_ref, nused_ref, x_ref, wg_ref, wu_ref, wd_ref, y_ref, x2d_scr):
    del blk_e_ref

    @pl.when(pl.program_id(0) < nused_ref[0])
    def _():
        x2d_scr[...] = x_ref[...].reshape(x2d_scr.shape)
        x = x2d_scr[...].astype(BF16)
        g = jnp.dot(x, wg_ref[0], preferred_element_type=F32)
        u = jnp.dot(x, wu_ref[0], preferred_element_type=F32)
        h = (g * jax.nn.sigmoid(g)) * u
        y = jnp.dot(h.astype(BF16), wd_ref[0], preferred_element_type=F32)
        y_ref[...] = y.reshape(y_ref.shape)

    @pl.when(pl.program_id(0) >= nused_ref[0])
    def _():
        y_ref[...] = jnp.zeros_like(y_ref)


def _experts(xs, wg, wu, wd, blk_e, nused):
    n_rows, _, d = xs.shape
    ff = wg.shape[2]
    nb = n_rows // MOE_BLOCK
    rows = lambda b, be, nu: (jnp.minimum(b, nu[0] - 1), 0, 0)
    out_rows = lambda b, be, nu: (b, 0, 0)
    return pl.pallas_call(
        _expert_kernel,
        grid_spec=pltpu.PrefetchScalarGridSpec(
            num_scalar_prefetch=2,
            grid=(nb,),
            in_specs=[pl.BlockSpec((MOE_BLOCK, 1, d), rows),
                      pl.BlockSpec((1, d, ff), lambda b, be, nu: (be[b], 0, 0)),
                      pl.BlockSpec((1, d, ff), lambda b, be, nu: (be[b], 0, 0)),
                      pl.BlockSpec((1, ff, d), lambda b, be, nu: (be[b], 0, 0))],
            out_specs=pl.BlockSpec((MOE_BLOCK, 1, d), out_rows),
            scratch_shapes=[pltpu.VMEM((MOE_BLOCK, d), F32)]),
        out_shape=jax.ShapeDtypeStruct((n_rows, 1, d), F32),
        compiler_params=_params(("arbitrary",)),
        name="experts",
    )(blk_e, nused, xs, wg, wu, wd)


NCH_GU = 4
NCH_D = 4


def _experts_streamed_kernel(nused_ref, blk_e_ref, first_ref, slot_ref, nxt_ref, lo_gu_ref, hi_gu_ref, lo_d_ref,
                             hi_d_ref, x_ref, wg_hbm, wu_hbm, wd_hbm, y_ref,
                             wg_bf, wu_bf, wd_bf, stg_g, stg_u, stg_d, x2d_scr, sems):
    b = pl.program_id(0)
    d, ff = wg_bf.shape[1], wg_bf.shape[2]
    ch_gu, ch_d = d // NCH_GU, ff // NCH_D
    mats = ((wg_hbm, stg_g, wg_bf, ch_gu, NCH_GU), (wu_hbm, stg_u, wu_bf, ch_gu, NCH_GU),
            (wd_hbm, stg_d, wd_bf, ch_d, NCH_D))

    def piece_copy(m, e, c):
        src, stg, _, ch, _ = mats[m]
        return pltpu.make_async_copy(src.at[e, pl.ds(pl.multiple_of(c * ch, ch), ch), :], stg, sems.at[m])

    def stream(m, e, slot, lo, hi):
        _, stg, dst, ch, nch = mats[m]

        def step(c, carry):
            piece_copy(m, e, c).wait()
            dst[slot, pl.ds(pl.multiple_of(c * ch, ch), ch), :] = stg[...].astype(BF16)

            @pl.when(c + 1 < nch)
            def _():
                piece_copy(m, e, c + 1).start()

            return carry

        lax.fori_loop(lo, hi, step, 0)

    @pl.when(b < nused_ref[0])
    def _():
        e_next = nxt_ref[b]
        slot = slot_ref[b]
        other = 1 - slot

        @pl.when(b == 0)
        def _():
            for m in range(3):
                piece_copy(m, blk_e_ref[0], 0).start()
            for m in range(3):
                stream(m, blk_e_ref[0], slot, 0, mats[m][4])

        @pl.when(jnp.logical_and(first_ref[b] == 1, e_next >= 0))
        def _():
            for m in range(3):
                piece_copy(m, e_next, 0).start()

        def prefetch(m, lo_ref, hi_ref):
            @pl.when(e_next >= 0)
            def _():
                stream(m, e_next, other, lo_ref[b], hi_ref[b])

        x2d_scr[...] = x_ref[...].reshape(x2d_scr.shape)
        x = x2d_scr[...].astype(BF16)
        g = jnp.dot(x, wg_bf[slot], preferred_element_type=F32)
        u = jnp.dot(x, wu_bf[slot], preferred_element_type=F32)
        h = (g * jax.nn.sigmoid(g)) * u
        y = jnp.dot(h.astype(BF16), wd_bf[slot], preferred_element_type=F32)
        y_ref[...] = y.reshape(y_ref.shape)
        prefetch(0, lo_gu_ref, hi_gu_ref)
        prefetch(1, lo_gu_ref, hi_gu_ref)
        prefetch(2, lo_d_ref, hi_d_ref)

    @pl.when(b >= nused_ref[0])
    def _():
        y_ref[...] = jnp.zeros_like(y_ref)


def _stream_schedule(padded, nused, n_blocks):
    i32 = jnp.int32
    nblk = (padded // MOE_BLOCK).astype(i32)
    bend = jnp.cumsum(nblk)
    bstart = bend - nblk
    blk = jnp.minimum(jnp.arange(n_blocks, dtype=i32), nused - 1)
    blk_e = jnp.minimum(jnp.sum(bend[None, :] <= blk[:, None], axis=1), N_EXPERTS - 1).astype(i32)
    r = blk - bstart[blk_e]
    nr = jnp.maximum(nblk[blk_e], 1)
    e_ids = jnp.arange(N_EXPERTS, dtype=i32)
    live = jnp.where(nblk > 0, e_ids, N_EXPERTS)
    suffix = lax.cummin(live[::-1])[::-1]
    nxt = jnp.concatenate([suffix[1:], jnp.full((1,), N_EXPERTS, i32)])
    nxt = jnp.where(nxt >= N_EXPERTS, -1, nxt)
    ordinal = jnp.cumsum((nblk > 0).astype(i32)) - 1
    first = (r == 0).astype(i32)
    slot = (ordinal[blk_e] % 2).astype(i32)
    pieces = lambda nch: ((nch * r) // nr, (nch * (r + 1)) // nr)
    lo_gu, hi_gu = pieces(NCH_GU)
    lo_d, hi_d = pieces(NCH_D)
    return blk_e, first, slot, nxt[blk_e], lo_gu.astype(i32), hi_gu.astype(i32), lo_d.astype(i32), hi_d.astype(i32)


def _experts_streamed(xs, wg, wu, wd, padded, nused):
    n_rows, _, d = xs.shape
    ff = wg.shape[2]
    nb = n_rows // MOE_BLOCK
    sched = _stream_schedule(padded, nused[0], nb)
    rows = lambda b, nu, *_: (jnp.minimum(b, nu[0] - 1), 0, 0)
    hbm = pl.BlockSpec(memory_space=pl.ANY)
    return pl.pallas_call(
        _experts_streamed_kernel,
        grid_spec=pltpu.PrefetchScalarGridSpec(
            num_scalar_prefetch=9,
            grid=(nb,),
            in_specs=[pl.BlockSpec((MOE_BLOCK, 1, d), rows), hbm, hbm, hbm],
            out_specs=pl.BlockSpec((MOE_BLOCK, 1, d), lambda b, *_: (b, 0, 0)),
            scratch_shapes=[pltpu.VMEM((2, d, ff), BF16), pltpu.VMEM((2, d, ff), BF16), pltpu.VMEM((2, ff, d), BF16),
                            pltpu.VMEM((d // NCH_GU, ff), F32), pltpu.VMEM((d // NCH_GU, ff), F32),
                            pltpu.VMEM((ff // NCH_D, d), F32),
                            pltpu.VMEM((MOE_BLOCK, d), F32),
                            pltpu.SemaphoreType.DMA((3,))]),
        out_shape=jax.ShapeDtypeStruct((n_rows, 1, d), F32),
        compiler_params=_params(("arbitrary",)),
        name="experts_routed",
    )(nused, *sched, xs, wg, wu, wd)


def _combine_kernel(dest_ref, wts_ref, ysh_ref, x1_ref, gt_ref, gpost_ref, yr_ref,
                    out_ref, buf, row2d_scr, sem):
    t = x1_ref.shape[0]

    def copy(tok, k):
        return _row_copy(yr_ref, dest_ref[0, k, tok], buf.at[k], tok, sem)

    def start(tok, c):
        for k in range(TOP_K):
            copy(tok, k).start()
        return c

    def wait(tok, c):
        for k in range(TOP_K):
            _row_copy(yr_ref, 0, buf.at[k], 0, sem).wait()
        return c

    lax.fori_loop(0, t, start, 0, unroll=ROW_DMA_UNROLL)
    lax.fori_loop(0, t, wait, 0, unroll=ROW_DMA_UNROLL)
    row2d_scr[...] = ysh_ref[...].reshape(row2d_scr.shape)
    y = row2d_scr[...]
    for k in range(TOP_K):
        row2d_scr[...] = buf[k].reshape(row2d_scr.shape)
        y = y + wts_ref[:, k:k + 1] * row2d_scr[...]
    out_ref[...] = x1_ref[...] + gt_ref[0] * (_rms(y) * gpost_ref[...])


def _combine(yr, dest3, wts, ysh, x1, gt_f, g_post, seq_len):
    n, d = x1.shape
    t = MOE_BLOCK
    tiles_per_seq = seq_len // t
    smem3 = pl.BlockSpec((1, SUBLANES, t), lambda i, *_: (i, 0, 0), memory_space=pltpu.SMEM)
    row = lambda i, *_: (i, 0)
    return pl.pallas_call(
        _combine_kernel,
        grid_spec=pltpu.PrefetchScalarGridSpec(
            num_scalar_prefetch=0,
            grid=(n // t,),
            in_specs=[smem3,
                      pl.BlockSpec((t, SUBLANES), row),
                      pl.BlockSpec((t, 1, d), lambda i, *_: (i, 0, 0)),
                      pl.BlockSpec((t, d), row),
                      pl.BlockSpec((1, 1, d), lambda i, *_: (i // tiles_per_seq, 0, 0)),
                      pl.BlockSpec((1, d), lambda i, *_: (0, 0)),
                      pl.BlockSpec(memory_space=pl.ANY)],
            out_specs=pl.BlockSpec((t, d), row),
            scratch_shapes=[pltpu.VMEM((TOP_K, t, 1, d), F32), pltpu.VMEM((t, d), F32),
                            pltpu.SemaphoreType.DMA(())]),
        out_shape=jax.ShapeDtypeStruct((n, d), F32),
        compiler_params=_params(("arbitrary",)),
        name="combine",
    )(dest3, wts, ysh, x1, gt_f, g_post.reshape(1, d), yr)


def _layer(x2, c, w_mod, b_mod, g_pre_mix, g_post_mix, g_pre_ffn, g_post_ffn, w_in, q_norm_g, k_norm_g,
           w_pool, pool_scale, w_out, w_router, router_bias, w_exp_gate, w_exp_up, w_exp_down,
           w_sh_gate, w_sh_up, w_sh_down, bsz, seq_len):
    n, d = x2.shape
    mod = _mod(c, w_mod, b_mod).reshape(bsz, N_MOD, 1, d)
    sh_a, sc_a, gt_a, sh_f, sc_f, gt_f = (mod[:, i] for i in range(N_MOD))

    qkv, pgg = _inproj(x2, g_pre_mix, sc_a, sh_a, w_in.astype(BF16), q_norm_g, k_norm_g, seq_len)
    attn = _attention(qkv, q_norm_g, k_norm_g, bsz, seq_len, d)
    x1, h2 = _mix(attn, pgg, x2, gt_a, sc_f, sh_f, w_pool.astype(BF16), pool_scale, w_out.astype(BF16),
                  g_post_mix, g_pre_ffn, seq_len)

    idx3, rank3, wts3, cnt = _router(h2, w_router, router_bias)
    counts = cnt[:, 0]
    padded = ((counts + MOE_BLOCK - 1) // MOE_BLOCK) * MOE_BLOCK
    pend = jnp.cumsum(padded)
    pstart = (pend - padded).astype(jnp.int32)
    n_rows = n * TOP_K + N_EXPERTS * MOE_BLOCK
    n_blocks = n_rows // MOE_BLOCK
    nused = (pend[-1] // MOE_BLOCK).astype(jnp.int32).reshape(1)
    all_blocks = jnp.arange(n_blocks, dtype=jnp.int32)
    padrow = jnp.concatenate([jnp.where(padded > 0, pend.astype(jnp.int32) - MOE_BLOCK, -1),
                              jnp.where(all_blocks >= nused[0], all_blocks * MOE_BLOCK, -1)])

    dest3 = rank3 + jnp.sum(jnp.where(idx3[..., None] == jnp.arange(N_EXPERTS, dtype=jnp.int32), pstart, 0), axis=-1)
    xs = _dispatch(h2, dest3, padrow, n_rows)
    yr = _experts_streamed(xs, w_exp_gate, w_exp_up, w_exp_down, padded, nused)
    ysh = _experts(h2, w_sh_gate.astype(BF16)[None], w_sh_up.astype(BF16)[None], w_sh_down.astype(BF16)[None],
                   jnp.zeros((n // MOE_BLOCK,), jnp.int32), jnp.full((1,), n // MOE_BLOCK, jnp.int32))
    wts = wts3.transpose(0, 2, 1).reshape(n, SUBLANES)
    return _combine(yr, dest3, wts, ysh, x1, gt_f, g_post_ffn, seq_len)


def kernel(x, c, w_mod, b_mod, g_pre_mix, g_post_mix, g_pre_ffn, g_post_ffn, w_in, q_norm_g, k_norm_g, w_pool,
           pool_scale, w_out, w_router, router_bias, w_exp_gate, w_exp_up, w_exp_down, w_sh_gate, w_sh_up,
           w_sh_down):
    bsz, seq_len, d = x.shape
    x2 = x.reshape(bsz * seq_len, d)
    for l in range(w_mod.shape[0]):
        x2 = _layer(x2, c, w_mod[l], b_mod[l], g_pre_mix[l], g_post_mix[l], g_pre_ffn[l], g_post_ffn[l],
                    w_in[l], q_norm_g[l], k_norm_g[l], w_pool[l], pool_scale[l], w_out[l], w_router[l],
                    router_bias[l], w_exp_gate[l], w_exp_up[l], w_exp_down[l], w_sh_gate[l], w_sh_up[l],
                    w_sh_down[l], bsz, seq_len)
    return x2.reshape(bsz, seq_len, d)
```

```python
import functools
import math

import jax
import jax.numpy as jnp
import numpy as np
from jax import lax
from jax.experimental import pallas as pl
from jax.experimental.pallas import tpu as pltpu

F32 = jnp.float32
BF16 = jnp.bfloat16

NORM_EPS = 1e-6
HEAD_DIM = 128
KV_GROUP = 4
GRID_W = 64
ROPE_THETA = 10000.0
ROPE_AXIS_DIM = HEAD_DIM // 2
POOL_WINDOWS = (2, 4, 8, 16)
POOL_HALO = 8
N_EXPERTS = 64
TOP_K = 6
N_EXPERT_GROUPS = 8
TOPK_GROUPS = 4
EXPERTS_PER_GROUP = N_EXPERTS // N_EXPERT_GROUPS
ROUTED_SCALE = 2.5
MOE_BLOCK = 256
N_MOD = 6
LANES = 128
SUBLANES = 8
VMEM_LIMIT = 56 * 1024 * 1024

Q_PRESCALE = (HEAD_DIM ** -0.5) * math.log2(math.e)


def _rms(x):
    return x * lax.rsqrt(jnp.mean(x * x, axis=-1, keepdims=True) + NORM_EPS)


def _params(sem, vmem=VMEM_LIMIT):
    return pltpu.CompilerParams(dimension_semantics=sem, vmem_limit_bytes=vmem)


def _mod_kernel(ct_ref, w_ref, b_ref, o_ref):
    ct = ct_ref[...]
    s = ct * jax.nn.sigmoid(ct)
    w = w_ref[...]
    rows = [jnp.sum(w * s[:, b:b + 1], axis=0, keepdims=True) for b in range(ct.shape[1])]
    o_ref[...] = jnp.concatenate(rows, axis=0) + b_ref[...]


def _mod(c, w_mod, b_mod):
    bsz, d = c.shape
    n = w_mod.shape[1]
    tn = min(512, n)
    return pl.pallas_call(
        _mod_kernel,
        grid=(n // tn,),
        in_specs=[pl.BlockSpec((d, bsz), lambda j: (0, 0)),
                  pl.BlockSpec((d, tn), lambda j: (0, j)),
                  pl.BlockSpec((1, tn), lambda j: (0, j))],
        out_specs=pl.BlockSpec((bsz, tn), lambda j: (0, j)),
        out_shape=jax.ShapeDtypeStruct((bsz, n), F32),
        compiler_params=_params(("arbitrary",)),
        name="mod",
    )(c.T, w_mod, b_mod.reshape(1, n))


def _rope_tables(seq_len):
    t = np.arange(seq_len)
    row = (t // GRID_W).astype(np.float32)
    col = (t % GRID_W).astype(np.float32)
    inv_freq = (ROPE_THETA ** (-np.arange(0, ROPE_AXIS_DIM, 2, dtype=np.float32) / ROPE_AXIS_DIM)).astype(np.float32)
    ang_r = row[:, None] * inv_freq[None, :]
    ang_c = col[:, None] * inv_freq[None, :]
    cr, sr, cc, sc = np.cos(ang_r), np.sin(ang_r), np.cos(ang_c), np.sin(ang_c)
    z = np.zeros_like(sr)
    cos = np.concatenate([cr, cr, cc, cc], axis=1)
    sa = np.concatenate([-sr, z, -sc, z], axis=1)
    sb = np.concatenate([z, sr, z, sc], axis=1)
    return tuple(jnp.asarray(a, F32) for a in (cos, sa, sb))


def _inproj_kernel(x_ref, g_ref, sc_ref, sh_ref, w_ref, gq_ref, gk_ref, cos_ref, sa_ref, sb_ref,
                   qkv_ref, f32_ref, h_scr, *, nq):
    j = pl.program_id(1)

    @pl.when(j == 0)
    def _():
        h = _rms(x_ref[...]) * g_ref[...]
        h = h * (1.0 + sc_ref[0]) + sh_ref[0]
        h_scr[...] = h.astype(BF16)

    acc = jnp.dot(h_scr[...], w_ref[...], preferred_element_type=F32)
    heads = acc.shape[1] // HEAD_DIM

    def qk_epilogue(gain):
        cos, sa, sb = cos_ref[...], sa_ref[...], sb_ref[...]
        for hh in range(heads):
            sl = slice(hh * HEAD_DIM, (hh + 1) * HEAD_DIM)
            y = _rms(acc[:, sl]) * gain
            r = y * cos + pltpu.roll(y, 3 * HEAD_DIM // 4, 1) * sa + pltpu.roll(y, HEAD_DIM // 4, 1) * sb
            qkv_ref[:, sl] = r.astype(BF16)

    @pl.when(j < nq)
    def _():
        qk_epilogue(gq_ref[...] * Q_PRESCALE)

    @pl.when(j == nq)
    def _():
        qk_epilogue(gk_ref[...])

    @pl.when(j == nq + 1)
    def _():
        qkv_ref[...] = acc.astype(BF16)

    @pl.when(j > nq + 1)
    def _():
        f32_ref[...] = acc


def _inproj(x2, g_pre, sc, sh, w_in_bf, gq, gk, seq_len):
    n, d = x2.shape
    kv_w = d // KV_GROUP
    tn = kv_w
    nq = d // tn
    n_qkv = nq + 2
    n_f32 = 3 * d // tn
    tm = min(1024, seq_len)
    tiles_per_seq = seq_len // tm
    cos, sa, sb = _rope_tables(seq_len)
    row = lambda i, j: (i, 0)
    bat = lambda i, j: (i // tiles_per_seq, 0, 0)
    pos = lambda i, j: (i % tiles_per_seq, 0)
    fixed = lambda i, j: (0, 0)
    return pl.pallas_call(
        functools.partial(_inproj_kernel, nq=nq),
        grid=(n // tm, n_qkv + n_f32),
        in_specs=[pl.BlockSpec((tm, d), row),
                  pl.BlockSpec((1, d), fixed),
                  pl.BlockSpec((1, 1, d), bat),
                  pl.BlockSpec((1, 1, d), bat),
                  pl.BlockSpec((d, tn), lambda i, j: (0, j)),
                  pl.BlockSpec((1, HEAD_DIM), fixed),
                  pl.BlockSpec((1, HEAD_DIM), fixed),
                  pl.BlockSpec((tm, HEAD_DIM), pos),
                  pl.BlockSpec((tm, HEAD_DIM), pos),
                  pl.BlockSpec((tm, HEAD_DIM), pos)],
        out_specs=[pl.BlockSpec((tm, tn), lambda i, j: (i, jnp.minimum(j, n_qkv - 1))),
                   pl.BlockSpec((tm, tn), lambda i, j: (i, jnp.maximum(j - n_qkv, 0)))],
        out_shape=[jax.ShapeDtypeStruct((n, n_qkv * tn), BF16),
                   jax.ShapeDtypeStruct((n, n_f32 * tn), F32)],
        scratch_shapes=[pltpu.VMEM((tm, d), BF16)],
        compiler_params=_params(("arbitrary", "arbitrary")),
        name="inproj",
    )(x2, g_pre.reshape(1, d), sc, sh, w_in_bf, gq.reshape(1, HEAD_DIM), gk.reshape(1, HEAD_DIM),
      cos, sa, sb)


def _attn_kernel(q_ref, k_ref, v_ref, o_ref, *, tk):
    tq = q_ref.shape[0]
    seq_len = k_ref.shape[0]
    qs = jnp.concatenate([q_ref[:, g * HEAD_DIM:(g + 1) * HEAD_DIM] for g in range(KV_GROUP)], axis=0)
    rows = qs.shape[0]

    def body(c, carry):
        m, l, acc = carry
        start = pl.multiple_of(c * tk, tk)
        kc = k_ref[pl.ds(start, tk), :]
        vc = v_ref[pl.ds(start, tk), :]
        s = lax.dot_general(qs, kc, (((1,), (1,)), ((), ())), preferred_element_type=F32)
        m_new = jnp.maximum(m, jnp.max(s, axis=-1, keepdims=True))
        alpha = jnp.exp2(m - m_new)
        p = jnp.exp2(s - m_new)
        l = alpha * l + jnp.sum(p, axis=-1, keepdims=True)
        acc = alpha * acc + jnp.dot(p.astype(BF16), vc, preferred_element_type=F32)
        return m_new, l, acc

    init = (jnp.full((rows, 1), -jnp.inf, F32), jnp.zeros((rows, 1), F32), jnp.zeros((rows, HEAD_DIM), F32))
    _, l, acc = lax.fori_loop(0, seq_len // tk, body, init)
    o = acc / l
    for g in range(KV_GROUP):
        o_ref[:, g * HEAD_DIM:(g + 1) * HEAD_DIM] = o[g * tq:(g + 1) * tq].astype(BF16)


ATTN_SCORE_BUFS = 2


def _attn_bounded_kernel(q_ref, k_ref, v_ref, o_ref, *s_scr, tk):
    tq = q_ref.shape[0]
    seq_len = k_ref.shape[0]
    qs = jnp.concatenate([q_ref[:, g * HEAD_DIM:(g + 1) * HEAD_DIM] for g in range(KV_GROUP)], axis=0)
    rows = qs.shape[0]
    nchunks = seq_len // tk

    def qk(c, dst):
        kc = k_ref[c * tk:(c + 1) * tk, :]
        dst[...] = lax.dot_general(kc, qs, (((1,), (1,)), ((), ())), preferred_element_type=F32)

    def pv(c, src, l, acc):
        vc = v_ref[c * tk:(c + 1) * tk, :]
        pt = jnp.exp2(src[...])
        l = l + jnp.sum(pt.reshape(tk // SUBLANES, SUBLANES, rows), axis=0)
        acc = acc + lax.dot_general(vc, pt.astype(BF16), (((0,), (0,)), ((), ())),
                                    preferred_element_type=F32)
        return l, acc

    l, acc = jnp.zeros((SUBLANES, rows), F32), jnp.zeros((HEAD_DIM, rows), F32)
    qk(0, s_scr[0])
    for c in range(nchunks):
        if c + 1 < nchunks:
            qk(c + 1, s_scr[(c + 1) % ATTN_SCORE_BUFS])
        l, acc = pv(c, s_scr[c % ATTN_SCORE_BUFS], l, acc)
    o = (acc / jnp.sum(l, axis=0, keepdims=True)).T
    for g in range(KV_GROUP):
        o_ref[:, g * HEAD_DIM:(g + 1) * HEAD_DIM] = o[g * tq:(g + 1) * tq].astype(BF16)


ATTN_SCORE_BOUND = 60.0


def _attention(qkv, gq, gk, bsz, seq_len, d):
    n = bsz * seq_len
    n_kv = d // HEAD_DIM // KV_GROUP
    gw = KV_GROUP * HEAD_DIM
    tq = min(128, seq_len)
    tk = min(512, seq_len)
    qt = seq_len // tq
    k_col0 = d // HEAD_DIM
    v_col0 = k_col0 + n_kv

    def call(body, scratch):
        return pl.pallas_call(
            functools.partial(body, tk=tk),
            grid=(bsz, n_kv, qt),
            in_specs=[pl.BlockSpec((tq, gw), lambda b, h, i: (b * qt + i, h)),
                      pl.BlockSpec((seq_len, HEAD_DIM), lambda b, h, i: (b, k_col0 + h)),
                      pl.BlockSpec((seq_len, HEAD_DIM), lambda b, h, i: (b, v_col0 + h))],
            out_specs=pl.BlockSpec((tq, gw), lambda b, h, i: (b * qt + i, h)),
            out_shape=jax.ShapeDtypeStruct((n, d), BF16),
            scratch_shapes=scratch,
            compiler_params=_params(("arbitrary", "arbitrary", "arbitrary")),
            name="attn",
        )(qkv, qkv, qkv)

    online = lambda: call(_attn_kernel, [])
    bound = 1.02 * HEAD_DIM * Q_PRESCALE * jnp.max(jnp.abs(gq)) * jnp.max(jnp.abs(gk))
    bounded = lambda: call(_attn_bounded_kernel, [pltpu.VMEM((tk, KV_GROUP * tq), F32)] * ATTN_SCORE_BUFS)
    return lax.cond(bound <= ATTN_SCORE_BOUND, bounded, online)


def _mix_kernel(attn_ref, p_ref, pprev_ref, pnext_ref, ga_ref, gp_ref, x_ref, gt_ref, scf_ref, shf_ref,
                wpool_ref, pscale_ref, wout_ref, gpost_ref, gpre_ref,
                x1_ref, h2_ref, pext_scr, merged_scr, *, seq_len):
    tm, d = p_ref.shape
    gw = d // len(POOL_WINDOWS)
    tiles_per_seq = seq_len // tm
    ti = pl.program_id(0) % tiles_per_seq
    pext_scr[0:POOL_HALO, :] = jnp.where(ti > 0, pprev_ref[...], 0.0)
    pext_scr[POOL_HALO:POOL_HALO + tm, :] = p_ref[...]
    pext_scr[POOL_HALO + tm:2 * POOL_HALO + tm, :] = jnp.where(ti < tiles_per_seq - 1, pnext_ref[...], 0.0)
    tpos = ti * tm + lax.broadcasted_iota(jnp.int32, (tm, 1), 0)
    for gi, w in enumerate(POOL_WINDOWS):
        cs = slice(gi * gw, (gi + 1) * gw)
        win = pext_scr[POOL_HALO - w // 2:POOL_HALO - w // 2 + tm, cs]
        for o in range(1 - w // 2, w // 2):
            win = win + pext_scr[POOL_HALO + o:POOL_HALO + o + tm, cs]
        cnt = (jnp.minimum(tpos + w // 2, seq_len) - jnp.maximum(tpos - w // 2, 0)).astype(F32)
        pooled = win / cnt - p_ref[:, cs]
        mixed = jnp.dot(pooled.astype(BF16), wpool_ref[gi], preferred_element_type=F32) * pscale_ref[:, cs]
        merged = (jax.nn.sigmoid(ga_ref[:, cs]) * attn_ref[:, cs].astype(F32)
                  + jax.nn.sigmoid(gp_ref[:, cs]) * mixed)
        merged_scr[:, cs] = merged.astype(BF16)
    y = jnp.dot(merged_scr[...], wout_ref[...], preferred_element_type=F32)
    x1 = x_ref[...] + gt_ref[0] * (_rms(y) * gpost_ref[...])
    x1_ref[...] = x1
    h2 = (_rms(x1) * gpre_ref[...]) * (1.0 + scf_ref[0]) + shf_ref[0]
    h2_ref[...] = h2.reshape(h2_ref.shape)


def _mix(attn, pgg, x2, gt_a, sc_f, sh_f, w_pool_bf, pool_scale, w_out_bf, g_post, g_pre_ffn, seq_len):
    n, d = x2.shape
    tm = min(256, seq_len)
    tiles_per_seq = seq_len // tm
    hb = tm // POOL_HALO
    n_hb = n // POOL_HALO
    row = lambda i: (i, 0)
    bat = lambda i: (i // tiles_per_seq, 0, 0)
    fixed2 = lambda i: (0, 0)
    ng = len(POOL_WINDOWS)
    gw = d // ng
    return pl.pallas_call(
        functools.partial(_mix_kernel, seq_len=seq_len),
        grid=(n // tm,),
        in_specs=[pl.BlockSpec((tm, d), row),
                  pl.BlockSpec((tm, d), lambda i: (i, 0)),
                  pl.BlockSpec((POOL_HALO, d), lambda i: (jnp.maximum(i * hb - 1, 0), 0)),
                  pl.BlockSpec((POOL_HALO, d), lambda i: (jnp.minimum((i + 1) * hb, n_hb - 1), 0)),
                  pl.BlockSpec((tm, d), lambda i: (i, 1)),
                  pl.BlockSpec((tm, d), lambda i: (i, 2)),
                  pl.BlockSpec((tm, d), row),
                  pl.BlockSpec((1, 1, d), bat),
                  pl.BlockSpec((1, 1, d), bat),
                  pl.BlockSpec((1, 1, d), bat),
                  pl.BlockSpec((ng, gw, gw), lambda i: (0, 0, 0)),
                  pl.BlockSpec((1, d), fixed2),
                  pl.BlockSpec((d, d), fixed2),
                  pl.BlockSpec((1, d), fixed2),
                  pl.BlockSpec((1, d), fixed2)],
        out_specs=[pl.BlockSpec((tm, d), row), pl.BlockSpec((tm, 1, d), lambda i: (i, 0, 0))],
        out_shape=[jax.ShapeDtypeStruct((n, d), F32), jax.ShapeDtypeStruct((n, 1, d), F32)],
        scratch_shapes=[pltpu.VMEM((tm + 2 * POOL_HALO, d), F32), pltpu.VMEM((tm, d), BF16)],
        compiler_params=_params(("arbitrary",)),
        name="mix",
    )(attn, pgg, pgg, pgg, pgg, pgg, x2, gt_a, sc_f, sh_f, w_pool_bf, pool_scale.reshape(1, d), w_out_bf,
      g_post.reshape(1, d), g_pre_ffn.reshape(1, d))


def _router_kernel(h_ref, wrt_ref, bias_ref, tri_ref, idx_ref, rank_ref, wts_ref, cnt_ref, carry_scr, h2d_scr):
    t = h_ref.shape[0]
    neg = -jnp.inf

    @pl.when(pl.program_id(0) == 0)
    def _():
        carry_scr[...] = jnp.zeros_like(carry_scr)

    h2d_scr[...] = h_ref[...].reshape(h2d_scr.shape)
    logits = lax.dot_general(wrt_ref[...], h2d_scr[...], (((1,), (1,)), ((), ())),
                             precision=lax.Precision.HIGHEST, preferred_element_type=F32)
    scores = jax.nn.sigmoid(logits)
    choice = scores + bias_ref[...]
    iota8 = lax.broadcasted_iota(jnp.int32, (SUBLANES, t), 0)
    iota_e = lax.broadcasted_iota(jnp.int32, (N_EXPERTS, t), 0)

    gs = jnp.full((N_EXPERT_GROUPS, t), neg, F32)
    for g in range(N_EXPERT_GROUPS):
        blk = choice[g * EXPERTS_PER_GROUP:(g + 1) * EXPERTS_PER_GROUP]
        m1 = jnp.max(blk, axis=0, keepdims=True)
        first = jnp.min(jnp.where(blk == m1, iota8, EXPERTS_PER_GROUP), axis=0, keepdims=True)
        m2 = jnp.max(jnp.where(iota8 == first, neg, blk), axis=0, keepdims=True)
        gs = jnp.where(iota8 == g, m1 + m2, gs)

    gmask = jnp.zeros((N_EXPERT_GROUPS, t), F32)
    for _ in range(TOPK_GROUPS):
        m = jnp.max(gs, axis=0, keepdims=True)
        first = jnp.min(jnp.where(gs == m, iota8, N_EXPERT_GROUPS), axis=0, keepdims=True)
        hit = iota8 == first
        gmask = jnp.where(hit, 1.0, gmask)
        gs = jnp.where(hit, neg, gs)

    masked = jnp.concatenate(
        [jnp.where(jnp.max(jnp.where(iota8 == g, gmask, 0.0), axis=0, keepdims=True) > 0.0,
                   choice[g * EXPERTS_PER_GROUP:(g + 1) * EXPERTS_PER_GROUP], neg)
         for g in range(N_EXPERT_GROUPS)], axis=0)

    selmask = jnp.zeros((N_EXPERTS, t), F32)
    idx_rows, sel_rows = [], []
    for _ in range(TOP_K):
        m = jnp.max(masked, axis=0, keepdims=True)
        first = jnp.min(jnp.where(masked == m, iota_e, N_EXPERTS), axis=0, keepdims=True)
        hit = iota_e == first
        idx_rows.append(first)
        sel_rows.append(jnp.sum(jnp.where(hit, scores, 0.0), axis=0, keepdims=True))
        selmask = jnp.where(hit, 1.0, selmask)
        masked = jnp.where(hit, neg, masked)
    denom = sel_rows[0]
    for r in sel_rows[1:]:
        denom = denom + r

    prefix = jnp.dot(selmask.astype(BF16), tri_ref[...], preferred_element_type=F32)
    rankfull = prefix + carry_scr[...]
    carry = carry_scr[...] + jnp.sum(selmask, axis=1, keepdims=True)
    carry_scr[...] = carry
    cnt_ref[...] = jnp.broadcast_to(carry, cnt_ref.shape).astype(jnp.int32)

    idx_out = jnp.zeros((SUBLANES, t), jnp.int32)
    rank_out = jnp.zeros((SUBLANES, t), jnp.int32)
    wts_out = jnp.zeros((SUBLANES, t), F32)
    for k in range(TOP_K):
        rk = jnp.sum(jnp.where(iota_e == idx_rows[k], rankfull, 0.0), axis=0, keepdims=True)
        idx_out = jnp.where(iota8 == k, idx_rows[k], idx_out)
        rank_out = jnp.where(iota8 == k, rk.astype(jnp.int32), rank_out)
        wts_out = jnp.where(iota8 == k, sel_rows[k] / denom * ROUTED_SCALE, wts_out)
    idx_ref[0] = idx_out
    rank_ref[0] = rank_out
    wts_ref[0] = wts_out


def _router(h2, w_router, router_bias):
    n, _, d = h2.shape
    t = MOE_BLOCK
    nt = n // t
    tri = (jnp.arange(t)[:, None] < jnp.arange(t)[None, :]).astype(BF16)
    blk3 = pl.BlockSpec((1, SUBLANES, t), lambda i: (i, 0, 0))
    return pl.pallas_call(
        _router_kernel,
        grid=(nt,),
        in_specs=[pl.BlockSpec((t, 1, d), lambda i: (i, 0, 0)),
                  pl.BlockSpec((N_EXPERTS, d), lambda i: (0, 0)),
                  pl.BlockSpec((N_EXPERTS, 1), lambda i: (0, 0)),
                  pl.BlockSpec((t, t), lambda i: (0, 0))],
        out_specs=[blk3, blk3, blk3, pl.BlockSpec((N_EXPERTS, LANES), lambda i: (0, 0))],
        out_shape=[jax.ShapeDtypeStruct((nt, SUBLANES, t), jnp.int32),
                   jax.ShapeDtypeStruct((nt, SUBLANES, t), jnp.int32),
                   jax.ShapeDtypeStruct((nt, SUBLANES, t), F32),
                   jax.ShapeDtypeStruct((N_EXPERTS, LANES), jnp.int32)],
        scratch_shapes=[pltpu.VMEM((N_EXPERTS, 1), F32), pltpu.VMEM((t, d), F32)],
        compiler_params=_params(("arbitrary",)),
        name="router",
    )(h2, w_router.T, router_bias.reshape(N_EXPERTS, 1), tri)


ROW_DMA_UNROLL = 4


def _row_copy(src_ref, src_row, dst_ref, dst_row, sem):
    return pltpu.make_async_copy(src_ref.at[pl.ds(src_row, 1)], dst_ref.at[pl.ds(dst_row, 1)], sem)


def _dispatch_shared_kernel(padrow_ref, dest_ref, h_ref, wg_ref, wu_ref, wd_ref, ysh_ref, xs_ref,
                            zero_scr, x2d_scr, sem):
    t = h_ref.shape[0]

    @pl.when(pl.program_id(0) == 0)
    def _():
        zero_scr[...] = jnp.zeros_like(zero_scr)
        n_cand = padrow_ref.shape[0]

        def zero_copy(e):
            return pltpu.make_async_copy(zero_scr, xs_ref.at[pl.ds(jnp.maximum(padrow_ref[e], 0), MOE_BLOCK)], sem)

        def start(e, c):
            @pl.when(padrow_ref[e] >= 0)
            def _():
                zero_copy(e).start()
            return c

        def wait(e, c):
            @pl.when(padrow_ref[e] >= 0)
            def _():
                zero_copy(e).wait()
            return c

        lax.fori_loop(0, n_cand, start, 0)
        lax.fori_loop(0, n_cand, wait, 0)

    def start(tok, c):
        for k in range(TOP_K):
            _row_copy(h_ref, tok, xs_ref, dest_ref[0, k, tok], sem).start()
        return c

    def wait(tok, c):
        for k in range(TOP_K):
            _row_copy(h_ref, 0, xs_ref, 0, sem).wait()
        return c

    lax.fori_loop(0, t, start, 0, unroll=ROW_DMA_UNROLL)

    x2d_scr[...] = h_ref[...].reshape(x2d_scr.shape)
    x = x2d_scr[...].astype(BF16)
    g = jnp.dot(x, wg_ref[...], preferred_element_type=F32)
    u = jnp.dot(x, wu_ref[...], preferred_element_type=F32)
    h = (g * jax.nn.sigmoid(g)) * u
    y = jnp.dot(h.astype(BF16), wd_ref[...], preferred_element_type=F32)
    ysh_ref[...] = y.reshape(ysh_ref.shape)

    lax.fori_loop(0, t, wait, 0, unroll=ROW_DMA_UNROLL)


def _dispatch_shared(h2, dest3, padrow, n_rows, wg, wu, wd):
    n, _, d = h2.shape
    ff = wg.shape[1]
    t = MOE_BLOCK
    smem3 = pl.BlockSpec((1, SUBLANES, t), lambda i, *_: (i, 0, 0), memory_space=pltpu.SMEM)
    rows = pl.BlockSpec((t, 1, d), lambda i, *_: (i, 0, 0))
    fixed = lambda shape: pl.BlockSpec(shape, lambda i, *_: (0, 0))
    return pl.pallas_call(
        _dispatch_shared_kernel,
        grid_spec=pltpu.PrefetchScalarGridSpec(
            num_scalar_prefetch=1,
            grid=(n // t,),
            in_specs=[smem3, rows, fixed((d, ff)), fixed((d, ff)), fixed((ff, d))],
            out_specs=[rows, pl.BlockSpec(memory_space=pl.ANY)],
            scratch_shapes=[pltpu.VMEM((MOE_BLOCK, 1, d), F32), pltpu.VMEM((t, d), F32),
                            pltpu.SemaphoreType.DMA(())]),
        out_shape=[jax.ShapeDtypeStruct((n, 1, d), F32), jax.ShapeDtypeStruct((n_rows, 1, d), F32)],
        compiler_params=_params(("arbitrary",)),
        name="dispatch_shared",
    )(padrow, dest3, h2, wg, wu, wd)


NCH_GU = 4
NCH_D = 4


def _experts_streamed_kernel(nused_ref, blk_e_ref, first_ref, slot_ref, nxt_ref, lo_gu_ref, hi_gu_ref, lo_d_ref,
                             hi_d_ref, x_ref, wg_hbm, wu_hbm, wd_hbm, y_ref,
                             wg_bf, wu_bf, wd_bf, stg_g, stg_u, stg_d, x2d_scr, sems):
    b = pl.program_id(0)
    d, ff = wg_bf.shape[1], wg_bf.shape[2]
    ch_gu, ch_d = d // NCH_GU, ff // NCH_D
    mats = ((wg_hbm, stg_g, wg_bf, ch_gu, NCH_GU), (wu_hbm, stg_u, wu_bf, ch_gu, NCH_GU),
            (wd_hbm, stg_d, wd_bf, ch_d, NCH_D))

    def piece_copy(m, e, c):
        src, stg, _, ch, _ = mats[m]
        return pltpu.make_async_copy(src.at[e, pl.ds(pl.multiple_of(c * ch, ch), ch), :], stg, sems.at[m])

    def stream(m, e, slot, lo, hi):
        _, stg, dst, ch, nch = mats[m]

        def step(c, carry):
            piece_copy(m, e, c).wait()
            dst[slot, pl.ds(pl.multiple_of(c * ch, ch), ch), :] = stg[...].astype(BF16)

            @pl.when(c + 1 < nch)
            def _():
                piece_copy(m, e, c + 1).start()

            return carry

        lax.fori_loop(lo, hi, step, 0)

    @pl.when(b < nused_ref[0])
    def _():
        e_next = nxt_ref[b]
        slot = slot_ref[b]
        other = 1 - slot

        @pl.when(b == 0)
        def _():
            for m in range(3):
                piece_copy(m, blk_e_ref[0], 0).start()
            for m in range(3):
                stream(m, blk_e_ref[0], slot, 0, mats[m][4])

        @pl.when(jnp.logical_and(first_ref[b] == 1, e_next >= 0))
        def _():
            for m in range(3):
                piece_copy(m, e_next, 0).start()

        def prefetch(m, lo_ref, hi_ref):
            @pl.when(e_next >= 0)
            def _():
                stream(m, e_next, other, lo_ref[b], hi_ref[b])

        x2d_scr[...] = x_ref[...].reshape(x2d_scr.shape)
        x = x2d_scr[...].astype(BF16)
        g = jnp.dot(x, wg_bf[slot], preferred_element_type=F32)
        u = jnp.dot(x, wu_bf[slot], preferred_element_type=F32)
        h = (g * jax.nn.sigmoid(g)) * u
        y = jnp.dot(h.astype(BF16), wd_bf[slot], preferred_element_type=F32)
        y_ref[...] = y.reshape(y_ref.shape)
        prefetch(0, lo_gu_ref, hi_gu_ref)
        prefetch(1, lo_gu_ref, hi_gu_ref)
        prefetch(2, lo_d_ref, hi_d_ref)

    @pl.when(b >= nused_ref[0])
    def _():
        y_ref[...] = jnp.zeros_like(y_ref)


def _stream_schedule(padded, nused, n_blocks):
    i32 = jnp.int32
    nblk = (padded // MOE_BLOCK).astype(i32)
    bend = jnp.cumsum(nblk)
    bstart = bend - nblk
    blk = jnp.minimum(jnp.arange(n_blocks, dtype=i32), nused - 1)
    blk_e = jnp.minimum(jnp.sum(bend[None, :] <= blk[:, None], axis=1), N_EXPERTS - 1).astype(i32)
    e_ids = jnp.arange(N_EXPERTS, dtype=i32)
    of_block = lambda table: jnp.sum(jnp.where(blk_e[:, None] == e_ids[None, :], table[None, :], 0), axis=1)
    r = blk - of_block(bstart)
    nr = jnp.maximum(of_block(nblk), 1)
    live = jnp.where(nblk > 0, e_ids, N_EXPERTS)
    suffix = lax.cummin(live[::-1])[::-1]
    nxt = jnp.concatenate([suffix[1:], jnp.full((1,), N_EXPERTS, i32)])
    nxt = jnp.where(nxt >= N_EXPERTS, -1, nxt)
    ordinal = jnp.cumsum((nblk > 0).astype(i32)) - 1
    first = (r == 0).astype(i32)
    slot = (of_block(ordinal) % 2).astype(i32)
    pieces = lambda nch: ((nch * r) // nr, (nch * (r + 1)) // nr)
    lo_gu, hi_gu = pieces(NCH_GU)
    lo_d, hi_d = pieces(NCH_D)
    return blk_e, first, slot, of_block(nxt).astype(i32), lo_gu.astype(i32), hi_gu.astype(i32), lo_d.astype(i32), hi_d.astype(i32)


def _experts_streamed(xs, wg, wu, wd, padded, nused):
    n_rows, _, d = xs.shape
    ff = wg.shape[2]
    nb = n_rows // MOE_BLOCK
    sched = _stream_schedule(padded, nused[0], nb)
    rows = lambda b, nu, *_: (jnp.minimum(b, nu[0] - 1), 0, 0)
    hbm = pl.BlockSpec(memory_space=pl.ANY)
    return pl.pallas_call(
        _experts_streamed_kernel,
        grid_spec=pltpu.PrefetchScalarGridSpec(
            num_scalar_prefetch=9,
            grid=(nb,),
            in_specs=[pl.BlockSpec((MOE_BLOCK, 1, d), rows), hbm, hbm, hbm],
            out_specs=pl.BlockSpec((MOE_BLOCK, 1, d), lambda b, *_: (b, 0, 0)),
            scratch_shapes=[pltpu.VMEM((2, d, ff), BF16), pltpu.VMEM((2, d, ff), BF16), pltpu.VMEM((2, ff, d), BF16),
                            pltpu.VMEM((d // NCH_GU, ff), F32), pltpu.VMEM((d // NCH_GU, ff), F32),
                            pltpu.VMEM((ff // NCH_D, d), F32),
                            pltpu.VMEM((MOE_BLOCK, d), F32),
                            pltpu.SemaphoreType.DMA((3,))]),
        out_shape=jax.ShapeDtypeStruct((n_rows, 1, d), F32),
        compiler_params=_params(("arbitrary",)),
        name="experts_routed",
    )(nused, *sched, xs, wg, wu, wd)


def _combine_kernel(dest_ref, wts_ref, ysh_ref, x1_ref, gt_ref, gpost_ref, yr_ref,
                    out_ref, buf, row2d_scr, sem):
    t = x1_ref.shape[0]

    def copy(tok, k):
        return _row_copy(yr_ref, dest_ref[0, k, tok], buf.at[k], tok, sem)

    def start(tok, c):
        for k in range(TOP_K):
            copy(tok, k).start()
        return c

    def wait(tok, c):
        for k in range(TOP_K):
            _row_copy(yr_ref, 0, buf.at[k], 0, sem).wait()
        return c

    lax.fori_loop(0, t, start, 0, unroll=ROW_DMA_UNROLL)
    lax.fori_loop(0, t, wait, 0, unroll=ROW_DMA_UNROLL)
    row2d_scr[...] = ysh_ref[...].reshape(row2d_scr.shape)
    y = row2d_scr[...]
    for k in range(TOP_K):
        row2d_scr[...] = buf[k].reshape(row2d_scr.shape)
        y = y + wts_ref[:, k:k + 1] * row2d_scr[...]
    out_ref[...] = x1_ref[...] + gt_ref[0] * (_rms(y) * gpost_ref[...])


def _combine(yr, dest3, wts, ysh, x1, gt_f, g_post, seq_len):
    n, d = x1.shape
    t = MOE_BLOCK
    tiles_per_seq = seq_len // t
    smem3 = pl.BlockSpec((1, SUBLANES, t), lambda i, *_: (i, 0, 0), memory_space=pltpu.SMEM)
    row = lambda i, *_: (i, 0)
    return pl.pallas_call(
        _combine_kernel,
        grid_spec=pltpu.PrefetchScalarGridSpec(
            num_scalar_prefetch=0,
            grid=(n // t,),
            in_specs=[smem3,
                      pl.BlockSpec((t, SUBLANES), row),
                      pl.BlockSpec((t, 1, d), lambda i, *_: (i, 0, 0)),
                      pl.BlockSpec((t, d), row),
                      pl.BlockSpec((1, 1, d), lambda i, *_: (i // tiles_per_seq, 0, 0)),
                      pl.BlockSpec((1, d), lambda i, *_: (0, 0)),
                      pl.BlockSpec(memory_space=pl.ANY)],
            out_specs=pl.BlockSpec((t, d), row),
            scratch_shapes=[pltpu.VMEM((TOP_K, t, 1, d), F32), pltpu.VMEM((t, d), F32),
                            pltpu.SemaphoreType.DMA(())]),
        out_shape=jax.ShapeDtypeStruct((n, d), F32),
        compiler_params=_params(("arbitrary",)),
        name="combine",
    )(dest3, wts, ysh, x1, gt_f, g_post.reshape(1, d), yr)


def _layer(x2, c, w_mod, b_mod, g_pre_mix, g_post_mix, g_pre_ffn, g_post_ffn, w_in, q_norm_g, k_norm_g,
           w_pool, pool_scale, w_out, w_router, router_bias, w_exp_gate, w_exp_up, w_exp_down,
           w_sh_gate, w_sh_up, w_sh_down, bsz, seq_len):
    n, d = x2.shape
    mod = _mod(c, w_mod, b_mod).reshape(bsz, N_MOD, 1, d)
    sh_a, sc_a, gt_a, sh_f, sc_f, gt_f = (mod[:, i] for i in range(N_MOD))

    qkv, pgg = _inproj(x2, g_pre_mix, sc_a, sh_a, w_in.astype(BF16), q_norm_g, k_norm_g, seq_len)
    attn = _attention(qkv, q_norm_g, k_norm_g, bsz, seq_len, d)
    x1, h2 = _mix(attn, pgg, x2, gt_a, sc_f, sh_f, w_pool.astype(BF16), pool_scale, w_out.astype(BF16),
                  g_post_mix, g_pre_ffn, seq_len)

    idx3, rank3, wts3, cnt = _router(h2, w_router, router_bias)
    counts = cnt[:, 0]
    padded = ((counts + MOE_BLOCK - 1) // MOE_BLOCK) * MOE_BLOCK
    pend = jnp.cumsum(padded)
    pstart = (pend - padded).astype(jnp.int32)
    n_rows = n * TOP_K + N_EXPERTS * MOE_BLOCK
    n_blocks = n_rows // MOE_BLOCK
    nused = (pend[-1] // MOE_BLOCK).astype(jnp.int32).reshape(1)
    all_blocks = jnp.arange(n_blocks, dtype=jnp.int32)
    padrow = jnp.concatenate([jnp.where(padded > 0, pend.astype(jnp.int32) - MOE_BLOCK, -1),
                              jnp.where(all_blocks >= nused[0], all_blocks * MOE_BLOCK, -1)])

    dest3 = rank3 + jnp.sum(jnp.where(idx3[..., None] == jnp.arange(N_EXPERTS, dtype=jnp.int32), pstart, 0), axis=-1)
    ysh, xs = _dispatch_shared(h2, dest3, padrow, n_rows, w_sh_gate.astype(BF16), w_sh_up.astype(BF16),
                               w_sh_down.astype(BF16))
    yr = _experts_streamed(xs, w_exp_gate, w_exp_up, w_exp_down, padded, nused)
    wts = wts3.transpose(0, 2, 1).reshape(n, SUBLANES)
    return _combine(yr, dest3, wts, ysh, x1, gt_f, g_post_ffn, seq_len)


def kernel(x, c, w_mod, b_mod, g_pre_mix, g_post_mix, g_pre_ffn, g_post_ffn, w_in, q_norm_g, k_norm_g, w_pool,
           pool_scale, w_out, w_router, router_bias, w_exp_gate, w_exp_up, w_exp_down, w_sh_gate, w_sh_up,
           w_sh_down):
    bsz, seq_len, d = x.shape
    x2 = x.reshape(bsz * seq_len, d)
    for l in range(w_mod.shape[0]):
        x2 = _layer(x2, c, w_mod[l], b_mod[l], g_pre_mix[l], g_post_mix[l], g_pre_ffn[l], g_post_ffn[l],
                    w_in[l], q_norm_g[l], k_norm_g[l], w_pool[l], pool_scale[l], w_out[l], w_router[l],
                    router_bias[l], w_exp_gate[l], w_exp_up[l], w_exp_down[l], w_sh_gate[l], w_sh_up[l],
                    w_sh_down[l], bsz, seq_len)
    return x2.reshape(bsz, seq_len, d)
```

```python
import functools
import math

import jax
import jax.numpy as jnp
import numpy as np
from jax import lax
from jax.experimental import pallas as pl
from jax.experimental.pallas import tpu as pltpu

F32 = jnp.float32
BF16 = jnp.bfloat16

NORM_EPS = 1e-6
HEAD_DIM = 128
KV_GROUP = 4
GRID_W = 64
ROPE_THETA = 10000.0
ROPE_AXIS_DIM = HEAD_DIM // 2
POOL_WINDOWS = (2, 4, 8, 16)
POOL_HALO = 8
N_EXPERTS = 64
TOP_K = 6
N_EXPERT_GROUPS = 8
TOPK_GROUPS = 4
EXPERTS_PER_GROUP = N_EXPERTS // N_EXPERT_GROUPS
ROUTED_SCALE = 2.5
MOE_BLOCK = 256
N_MOD = 6
LANES = 128
SUBLANES = 8
VMEM_LIMIT = 56 * 1024 * 1024

Q_PRESCALE = (HEAD_DIM ** -0.5) * math.log2(math.e)


def _rms(x):
    return x * lax.rsqrt(jnp.mean(x * x, axis=-1, keepdims=True) + NORM_EPS)


def _params(sem, vmem=VMEM_LIMIT):
    return pltpu.CompilerParams(dimension_semantics=sem, vmem_limit_bytes=vmem)


def _mod_kernel(ct_ref, w_ref, b_ref, o_ref):
    ct = ct_ref[...]
    s = ct * jax.nn.sigmoid(ct)
    w = w_ref[...]
    rows = [jnp.sum(w * s[:, b:b + 1], axis=0, keepdims=True) for b in range(ct.shape[1])]
    o_ref[...] = jnp.concatenate(rows, axis=0) + b_ref[...]


def _mod(c, w_mod, b_mod):
    bsz, d = c.shape
    n = w_mod.shape[1]
    tn = min(512, n)
    return pl.pallas_call(
        _mod_kernel,
        grid=(n // tn,),
        in_specs=[pl.BlockSpec((d, bsz), lambda j: (0, 0)),
                  pl.BlockSpec((d, tn), lambda j: (0, j)),
                  pl.BlockSpec((1, tn), lambda j: (0, j))],
        out_specs=pl.BlockSpec((bsz, tn), lambda j: (0, j)),
        out_shape=jax.ShapeDtypeStruct((bsz, n), F32),
        compiler_params=_params(("arbitrary",)),
        name="mod",
    )(c.T, w_mod, b_mod.reshape(1, n))


def _rope_tables(seq_len):
    t = np.arange(seq_len)
    row = (t // GRID_W).astype(np.float32)
    col = (t % GRID_W).astype(np.float32)
    inv_freq = (ROPE_THETA ** (-np.arange(0, ROPE_AXIS_DIM, 2, dtype=np.float32) / ROPE_AXIS_DIM)).astype(np.float32)
    ang_r = row[:, None] * inv_freq[None, :]
    ang_c = col[:, None] * inv_freq[None, :]
    cr, sr, cc, sc = np.cos(ang_r), np.sin(ang_r), np.cos(ang_c), np.sin(ang_c)
    z = np.zeros_like(sr)
    cos = np.concatenate([cr, cr, cc, cc], axis=1)
    sa = np.concatenate([-sr, z, -sc, z], axis=1)
    sb = np.concatenate([z, sr, z, sc], axis=1)
    return tuple(jnp.asarray(a, F32) for a in (cos, sa, sb))


def _inproj_kernel(x_ref, g_ref, sc_ref, sh_ref, w_ref, gq_ref, gk_ref, cos_ref, sa_ref, sb_ref,
                   qkv_ref, f32_ref, h_scr, *, nq):
    j = pl.program_id(1)

    @pl.when(j == 0)
    def _():
        h = _rms(x_ref[...]) * g_ref[...]
        h = h * (1.0 + sc_ref[0]) + sh_ref[0]
        h_scr[...] = h.astype(BF16)

    acc = jnp.dot(h_scr[...], w_ref[...], preferred_element_type=F32)
    heads = acc.shape[1] // HEAD_DIM

    def qk_epilogue(gain):
        cos, sa, sb = cos_ref[...], sa_ref[...], sb_ref[...]
        for hh in range(heads):
            sl = slice(hh * HEAD_DIM, (hh + 1) * HEAD_DIM)
            y = _rms(acc[:, sl]) * gain
            r = y * cos + pltpu.roll(y, 3 * HEAD_DIM // 4, 1) * sa + pltpu.roll(y, HEAD_DIM // 4, 1) * sb
            qkv_ref[:, sl] = r.astype(BF16)

    @pl.when(j < nq)
    def _():
        qk_epilogue(gq_ref[...] * Q_PRESCALE)

    @pl.when(j == nq)
    def _():
        qk_epilogue(gk_ref[...])

    @pl.when(j == nq + 1)
    def _():
        qkv_ref[...] = acc.astype(BF16)

    @pl.when(j > nq + 1)
    def _():
        f32_ref[...] = acc


def _inproj(x2, g_pre, sc, sh, w_in_bf, gq, gk, seq_len):
    n, d = x2.shape
    kv_w = d // KV_GROUP
    tn = kv_w
    nq = d // tn
    n_qkv = nq + 2
    n_f32 = 3 * d // tn
    tm = min(1024, seq_len)
    tiles_per_seq = seq_len // tm
    cos, sa, sb = _rope_tables(seq_len)
    row = lambda i, j: (i, 0)
    bat = lambda i, j: (i // tiles_per_seq, 0, 0)
    pos = lambda i, j: (i % tiles_per_seq, 0)
    fixed = lambda i, j: (0, 0)
    return pl.pallas_call(
        functools.partial(_inproj_kernel, nq=nq),
        grid=(n // tm, n_qkv + n_f32),
        in_specs=[pl.BlockSpec((tm, d), row),
                  pl.BlockSpec((1, d), fixed),
                  pl.BlockSpec((1, 1, d), bat),
                  pl.BlockSpec((1, 1, d), bat),
                  pl.BlockSpec((d, tn), lambda i, j: (0, j)),
                  pl.BlockSpec((1, HEAD_DIM), fixed),
                  pl.BlockSpec((1, HEAD_DIM), fixed),
                  pl.BlockSpec((tm, HEAD_DIM), pos),
                  pl.BlockSpec((tm, HEAD_DIM), pos),
                  pl.BlockSpec((tm, HEAD_DIM), pos)],
        out_specs=[pl.BlockSpec((tm, tn), lambda i, j: (i, jnp.minimum(j, n_qkv - 1))),
                   pl.BlockSpec((tm, tn), lambda i, j: (i, jnp.maximum(j - n_qkv, 0)))],
        out_shape=[jax.ShapeDtypeStruct((n, n_qkv * tn), BF16),
                   jax.ShapeDtypeStruct((n, n_f32 * tn), F32)],
        scratch_shapes=[pltpu.VMEM((tm, d), BF16)],
        compiler_params=_params(("arbitrary", "arbitrary")),
        name="inproj",
    )(x2, g_pre.reshape(1, d), sc, sh, w_in_bf, gq.reshape(1, HEAD_DIM), gk.reshape(1, HEAD_DIM),
      cos, sa, sb)


def _attn_kernel(q_ref, k_ref, v_ref, o_ref, *, tk):
    tq = q_ref.shape[0]
    seq_len = k_ref.shape[0]
    qs = jnp.concatenate([q_ref[:, g * HEAD_DIM:(g + 1) * HEAD_DIM] for g in range(KV_GROUP)], axis=0)
    rows = qs.shape[0]

    def body(c, carry):
        m, l, acc = carry
        start = pl.multiple_of(c * tk, tk)
        kc = k_ref[pl.ds(start, tk), :]
        vc = v_ref[pl.ds(start, tk), :]
        s = lax.dot_general(qs, kc, (((1,), (1,)), ((), ())), preferred_element_type=F32)
        m_new = jnp.maximum(m, jnp.max(s, axis=-1, keepdims=True))
        alpha = jnp.exp2(m - m_new)
        p = jnp.exp2(s - m_new)
        l = alpha * l + jnp.sum(p, axis=-1, keepdims=True)
        acc = alpha * acc + jnp.dot(p.astype(BF16), vc, preferred_element_type=F32)
        return m_new, l, acc

    init = (jnp.full((rows, 1), -jnp.inf, F32), jnp.zeros((rows, 1), F32), jnp.zeros((rows, HEAD_DIM), F32))
    _, l, acc = lax.fori_loop(0, seq_len // tk, body, init)
    o = acc / l
    for g in range(KV_GROUP):
        o_ref[:, g * HEAD_DIM:(g + 1) * HEAD_DIM] = o[g * tq:(g + 1) * tq].astype(BF16)


ATTN_SCORE_BUFS = 2


def _attn_bounded_kernel(q_ref, k_ref, v_ref, o_ref, *s_scr, tk):
    tq = q_ref.shape[0]
    seq_len = k_ref.shape[0]
    qs = jnp.concatenate([q_ref[:, g * HEAD_DIM:(g + 1) * HEAD_DIM] for g in range(KV_GROUP)], axis=0)
    rows = qs.shape[0]
    nchunks = seq_len // tk

    def qk(c, dst):
        kc = k_ref[c * tk:(c + 1) * tk, :]
        dst[...] = lax.dot_general(kc, qs, (((1,), (1,)), ((), ())), preferred_element_type=F32)

    def pv(c, src, l, acc):
        vc = v_ref[c * tk:(c + 1) * tk, :]
        pt = jnp.exp2(src[...])
        l = l + jnp.sum(pt.reshape(tk // SUBLANES, SUBLANES, rows), axis=0)
        acc = acc + lax.dot_general(vc, pt.astype(BF16), (((0,), (0,)), ((), ())),
                                    preferred_element_type=F32)
        return l, acc

    l, acc = jnp.zeros((SUBLANES, rows), F32), jnp.zeros((HEAD_DIM, rows), F32)
    qk(0, s_scr[0])
    for c in range(nchunks):
        if c + 1 < nchunks:
            qk(c + 1, s_scr[(c + 1) % ATTN_SCORE_BUFS])
        l, acc = pv(c, s_scr[c % ATTN_SCORE_BUFS], l, acc)
    o = (acc / jnp.sum(l, axis=0, keepdims=True)).T
    for g in range(KV_GROUP):
        o_ref[:, g * HEAD_DIM:(g + 1) * HEAD_DIM] = o[g * tq:(g + 1) * tq].astype(BF16)


ATTN_SCORE_BOUND = 60.0


def _attention(qkv, gq, gk, bsz, seq_len, d):
    n = bsz * seq_len
    n_kv = d // HEAD_DIM // KV_GROUP
    gw = KV_GROUP * HEAD_DIM
    tk = min(512, seq_len)
    k_col0 = d // HEAD_DIM
    v_col0 = k_col0 + n_kv

    def call(body, tq, n_score_bufs):
        qt = seq_len // tq
        scratch = [pltpu.VMEM((tk, KV_GROUP * tq), F32)] * n_score_bufs
        return pl.pallas_call(
            functools.partial(body, tk=tk),
            grid=(bsz, n_kv, qt),
            in_specs=[pl.BlockSpec((tq, gw), lambda b, h, i: (b * qt + i, h)),
                      pl.BlockSpec((seq_len, HEAD_DIM), lambda b, h, i: (b, k_col0 + h)),
                      pl.BlockSpec((seq_len, HEAD_DIM), lambda b, h, i: (b, v_col0 + h))],
            out_specs=pl.BlockSpec((tq, gw), lambda b, h, i: (b * qt + i, h)),
            out_shape=jax.ShapeDtypeStruct((n, d), BF16),
            scratch_shapes=scratch,
            compiler_params=_params(("arbitrary", "arbitrary", "arbitrary")),
            name="attn",
        )(qkv, qkv, qkv)

    online = lambda: call(_attn_kernel, min(128, seq_len), 0)
    bound = 1.02 * HEAD_DIM * Q_PRESCALE * jnp.max(jnp.abs(gq)) * jnp.max(jnp.abs(gk))
    bounded = lambda: call(_attn_bounded_kernel, min(256, seq_len), ATTN_SCORE_BUFS)
    return lax.cond(bound <= ATTN_SCORE_BOUND, bounded, online)


def _mix_kernel(attn_ref, p_ref, pprev_ref, pnext_ref, ga_ref, gp_ref, x_ref, gt_ref, scf_ref, shf_ref,
                wpool_ref, pscale_ref, wout_ref, gpost_ref, gpre_ref,
                x1_ref, h2_ref, pext_scr, merged_scr, *, seq_len):
    tm, d = p_ref.shape
    gw = d // len(POOL_WINDOWS)
    tiles_per_seq = seq_len // tm
    ti = pl.program_id(0) % tiles_per_seq
    pext_scr[0:POOL_HALO, :] = jnp.where(ti > 0, pprev_ref[...], 0.0)
    pext_scr[POOL_HALO:POOL_HALO + tm, :] = p_ref[...]
    pext_scr[POOL_HALO + tm:2 * POOL_HALO + tm, :] = jnp.where(ti < tiles_per_seq - 1, pnext_ref[...], 0.0)
    tpos = ti * tm + lax.broadcasted_iota(jnp.int32, (tm, 1), 0)
    for gi, w in enumerate(POOL_WINDOWS):
        cs = slice(gi * gw, (gi + 1) * gw)
        win = pext_scr[POOL_HALO - w // 2:POOL_HALO - w // 2 + tm, cs]
        for o in range(1 - w // 2, w // 2):
            win = win + pext_scr[POOL_HALO + o:POOL_HALO + o + tm, cs]
        cnt = (jnp.minimum(tpos + w // 2, seq_len) - jnp.maximum(tpos - w // 2, 0)).astype(F32)
        pooled = win / cnt - p_ref[:, cs]
        mixed = jnp.dot(pooled.astype(BF16), wpool_ref[gi], preferred_element_type=F32) * pscale_ref[:, cs]
        merged = (jax.nn.sigmoid(ga_ref[:, cs]) * attn_ref[:, cs].astype(F32)
                  + jax.nn.sigmoid(gp_ref[:, cs]) * mixed)
        merged_scr[:, cs] = merged.astype(BF16)
    y = jnp.dot(merged_scr[...], wout_ref[...], preferred_element_type=F32)
    x1 = x_ref[...] + gt_ref[0] * (_rms(y) * gpost_ref[...])
    x1_ref[...] = x1
    h2 = (_rms(x1) * gpre_ref[...]) * (1.0 + scf_ref[0]) + shf_ref[0]
    h2_ref[...] = h2.reshape(h2_ref.shape)


def _mix(attn, pgg, x2, gt_a, sc_f, sh_f, w_pool_bf, pool_scale, w_out_bf, g_post, g_pre_ffn, seq_len):
    n, d = x2.shape
    tm = min(256, seq_len)
    tiles_per_seq = seq_len // tm
    hb = tm // POOL_HALO
    n_hb = n // POOL_HALO
    row = lambda i: (i, 0)
    bat = lambda i: (i // tiles_per_seq, 0, 0)
    fixed2 = lambda i: (0, 0)
    ng = len(POOL_WINDOWS)
    gw = d // ng
    return pl.pallas_call(
        functools.partial(_mix_kernel, seq_len=seq_len),
        grid=(n // tm,),
        in_specs=[pl.BlockSpec((tm, d), row),
                  pl.BlockSpec((tm, d), lambda i: (i, 0)),
                  pl.BlockSpec((POOL_HALO, d), lambda i: (jnp.maximum(i * hb - 1, 0), 0)),
                  pl.BlockSpec((POOL_HALO, d), lambda i: (jnp.minimum((i + 1) * hb, n_hb - 1), 0)),
                  pl.BlockSpec((tm, d), lambda i: (i, 1)),
                  pl.BlockSpec((tm, d), lambda i: (i, 2)),
                  pl.BlockSpec((tm, d), row),
                  pl.BlockSpec((1, 1, d), bat),
                  pl.BlockSpec((1, 1, d), bat),
                  pl.BlockSpec((1, 1, d), bat),
                  pl.BlockSpec((ng, gw, gw), lambda i: (0, 0, 0)),
                  pl.BlockSpec((1, d), fixed2),
                  pl.BlockSpec((d, d), fixed2),
                  pl.BlockSpec((1, d), fixed2),
                  pl.BlockSpec((1, d), fixed2)],
        out_specs=[pl.BlockSpec((tm, d), row), pl.BlockSpec((tm, 1, d), lambda i: (i, 0, 0))],
        out_shape=[jax.ShapeDtypeStruct((n, d), F32), jax.ShapeDtypeStruct((n, 1, d), F32)],
        scratch_shapes=[pltpu.VMEM((tm + 2 * POOL_HALO, d), F32), pltpu.VMEM((tm, d), BF16)],
        compiler_params=_params(("arbitrary",)),
        name="mix",
    )(attn, pgg, pgg, pgg, pgg, pgg, x2, gt_a, sc_f, sh_f, w_pool_bf, pool_scale.reshape(1, d), w_out_bf,
      g_post.reshape(1, d), g_pre_ffn.reshape(1, d))


def _router_kernel(h_ref, wrt_ref, bias_ref, tri_ref, idx_ref, rank_ref, wts_ref, cnt_ref, carry_scr, h2d_scr):
    t = h_ref.shape[0]
    neg = -jnp.inf

    @pl.when(pl.program_id(0) == 0)
    def _():
        carry_scr[...] = jnp.zeros_like(carry_scr)

    h2d_scr[...] = h_ref[...].reshape(h2d_scr.shape)
    logits = lax.dot_general(wrt_ref[...], h2d_scr[...], (((1,), (1,)), ((), ())),
                             precision=lax.Precision.HIGHEST, preferred_element_type=F32)
    scores = jax.nn.sigmoid(logits)
    choice = scores + bias_ref[...]
    iota8 = lax.broadcasted_iota(jnp.int32, (SUBLANES, t), 0)
    iota_e = lax.broadcasted_iota(jnp.int32, (N_EXPERTS, t), 0)

    gs = jnp.full((N_EXPERT_GROUPS, t), neg, F32)
    for g in range(N_EXPERT_GROUPS):
        blk = choice[g * EXPERTS_PER_GROUP:(g + 1) * EXPERTS_PER_GROUP]
        m1 = jnp.max(blk, axis=0, keepdims=True)
        first = jnp.min(jnp.where(blk == m1, iota8, EXPERTS_PER_GROUP), axis=0, keepdims=True)
        m2 = jnp.max(jnp.where(iota8 == first, neg, blk), axis=0, keepdims=True)
        gs = jnp.where(iota8 == g, m1 + m2, gs)

    gmask = jnp.zeros((N_EXPERT_GROUPS, t), F32)
    for _ in range(TOPK_GROUPS):
        m = jnp.max(gs, axis=0, keepdims=True)
        first = jnp.min(jnp.where(gs == m, iota8, N_EXPERT_GROUPS), axis=0, keepdims=True)
        hit = iota8 == first
        gmask = jnp.where(hit, 1.0, gmask)
        gs = jnp.where(hit, neg, gs)

    masked = jnp.concatenate(
        [jnp.where(jnp.max(jnp.where(iota8 == g, gmask, 0.0), axis=0, keepdims=True) > 0.0,
                   choice[g * EXPERTS_PER_GROUP:(g + 1) * EXPERTS_PER_GROUP], neg)
         for g in range(N_EXPERT_GROUPS)], axis=0)

    selmask = jnp.zeros((N_EXPERTS, t), F32)
    idx_rows, sel_rows = [], []
    for _ in range(TOP_K):
        m = jnp.max(masked, axis=0, keepdims=True)
        first = jnp.min(jnp.where(masked == m, iota_e, N_EXPERTS), axis=0, keepdims=True)
        hit = iota_e == first
        idx_rows.append(first)
        sel_rows.append(jnp.sum(jnp.where(hit, scores, 0.0), axis=0, keepdims=True))
        selmask = jnp.where(hit, 1.0, selmask)
        masked = jnp.where(hit, neg, masked)
    denom = sel_rows[0]
    for r in sel_rows[1:]:
        denom = denom + r

    prefix = jnp.dot(selmask.astype(BF16), tri_ref[...], preferred_element_type=F32)
    rankfull = prefix + carry_scr[...]
    carry = carry_scr[...] + jnp.sum(selmask, axis=1, keepdims=True)
    carry_scr[...] = carry
    cnt_ref[...] = jnp.broadcast_to(carry, cnt_ref.shape).astype(jnp.int32)

    idx_out = jnp.zeros((SUBLANES, t), jnp.int32)
    rank_out = jnp.zeros((SUBLANES, t), jnp.int32)
    wts_out = jnp.zeros((SUBLANES, t), F32)
    for k in range(TOP_K):
        rk = jnp.sum(jnp.where(iota_e == idx_rows[k], rankfull, 0.0), axis=0, keepdims=True)
        idx_out = jnp.where(iota8 == k, idx_rows[k], idx_out)
        rank_out = jnp.where(iota8 == k, rk.astype(jnp.int32), rank_out)
        wts_out = jnp.where(iota8 == k, sel_rows[k] / denom * ROUTED_SCALE, wts_out)
    idx_ref[0] = idx_out
    rank_ref[0] = rank_out
    wts_ref[0] = wts_out


def _router(h2, w_router, router_bias):
    n, _, d = h2.shape
    t = MOE_BLOCK
    nt = n // t
    tri = (jnp.arange(t)[:, None] < jnp.arange(t)[None, :]).astype(BF16)
    blk3 = pl.BlockSpec((1, SUBLANES, t), lambda i: (i, 0, 0))
    return pl.pallas_call(
        _router_kernel,
        grid=(nt,),
        in_specs=[pl.BlockSpec((t, 1, d), lambda i: (i, 0, 0)),
                  pl.BlockSpec((N_EXPERTS, d), lambda i: (0, 0)),
                  pl.BlockSpec((N_EXPERTS, 1), lambda i: (0, 0)),
                  pl.BlockSpec((t, t), lambda i: (0, 0))],
        out_specs=[blk3, blk3, blk3, pl.BlockSpec((N_EXPERTS, LANES), lambda i: (0, 0))],
        out_shape=[jax.ShapeDtypeStruct((nt, SUBLANES, t), jnp.int32),
                   jax.ShapeDtypeStruct((nt, SUBLANES, t), jnp.int32),
                   jax.ShapeDtypeStruct((nt, SUBLANES, t), F32),
                   jax.ShapeDtypeStruct((N_EXPERTS, LANES), jnp.int32)],
        scratch_shapes=[pltpu.VMEM((N_EXPERTS, 1), F32), pltpu.VMEM((t, d), F32)],
        compiler_params=_params(("arbitrary",)),
        name="router",
    )(h2, w_router.T, router_bias.reshape(N_EXPERTS, 1), tri)


ROW_DMA_UNROLL = 4


def _row_copy(src_ref, src_row, dst_ref, dst_row, sem):
    return pltpu.make_async_copy(src_ref.at[pl.ds(src_row, 1)], dst_ref.at[pl.ds(dst_row, 1)], sem)


def _dispatch_shared_kernel(padrow_ref, dest_ref, h_ref, wg_ref, wu_ref, wd_ref, ysh_ref, xs_ref,
                            zero_scr, x2d_scr, sem):
    t = h_ref.shape[0]

    @pl.when(pl.program_id(0) == 0)
    def _():
        zero_scr[...] = jnp.zeros_like(zero_scr)
        n_cand = padrow_ref.shape[0]

        def zero_copy(e):
            return pltpu.make_async_copy(zero_scr, xs_ref.at[pl.ds(jnp.maximum(padrow_ref[e], 0), MOE_BLOCK)], sem)

        def start(e, c):
            @pl.when(padrow_ref[e] >= 0)
            def _():
                zero_copy(e).start()
            return c

        def wait(e, c):
            @pl.when(padrow_ref[e] >= 0)
            def _():
                zero_copy(e).wait()
            return c

        lax.fori_loop(0, n_cand, start, 0)
        lax.fori_loop(0, n_cand, wait, 0)

    def start(tok, c):
        for k in range(TOP_K):
            _row_copy(h_ref, tok, xs_ref, dest_ref[0, k, tok], sem).start()
        return c

    def wait(tok, c):
        for k in range(TOP_K):
            _row_copy(h_ref, 0, xs_ref, 0, sem).wait()
        return c

    lax.fori_loop(0, t, start, 0, unroll=ROW_DMA_UNROLL)

    x2d_scr[...] = h_ref[...].reshape(x2d_scr.shape)
    x = x2d_scr[...].astype(BF16)
    g = jnp.dot(x, wg_ref[...], preferred_element_type=F32)
    u = jnp.dot(x, wu_ref[...], preferred_element_type=F32)
    h = (g * jax.nn.sigmoid(g)) * u
    y = jnp.dot(h.astype(BF16), wd_ref[...], preferred_element_type=F32)
    ysh_ref[...] = y.reshape(ysh_ref.shape)

    lax.fori_loop(0, t, wait, 0, unroll=ROW_DMA_UNROLL)


def _dispatch_shared(h2, dest3, padrow, n_rows, wg, wu, wd):
    n, _, d = h2.shape
    ff = wg.shape[1]
    t = MOE_BLOCK
    smem3 = pl.BlockSpec((1, SUBLANES, t), lambda i, *_: (i, 0, 0), memory_space=pltpu.SMEM)
    rows = pl.BlockSpec((t, 1, d), lambda i, *_: (i, 0, 0))
    fixed = lambda shape: pl.BlockSpec(shape, lambda i, *_: (0, 0))
    return pl.pallas_call(
        _dispatch_shared_kernel,
        grid_spec=pltpu.PrefetchScalarGridSpec(
            num_scalar_prefetch=1,
            grid=(n // t,),
            in_specs=[smem3, rows, fixed((d, ff)), fixed((d, ff)), fixed((ff, d))],
            out_specs=[rows, pl.BlockSpec(memory_space=pl.ANY)],
            scratch_shapes=[pltpu.VMEM((MOE_BLOCK, 1, d), F32), pltpu.VMEM((t, d), F32),
                            pltpu.SemaphoreType.DMA(())]),
        out_shape=[jax.ShapeDtypeStruct((n, 1, d), F32), jax.ShapeDtypeStruct((n_rows, 1, d), F32)],
        compiler_params=_params(("arbitrary",)),
        name="dispatch_shared",
    )(padrow, dest3, h2, wg, wu, wd)


NCH_GU = 4
NCH_D = 4


def _experts_streamed_kernel(nused_ref, blk_e_ref, first_ref, slot_ref, nxt_ref, lo_gu_ref, hi_gu_ref, lo_d_ref,
                             hi_d_ref, x_ref, wg_hbm, wu_hbm, wd_hbm, y_ref,
                             wg_bf, wu_bf, wd_bf, stg_g, stg_u, stg_d, x2d_scr, sems):
    b = pl.program_id(0)
    d, ff = wg_bf.shape[1], wg_bf.shape[2]
    ch_gu, ch_d = d // NCH_GU, ff // NCH_D
    mats = ((wg_hbm, stg_g, wg_bf, ch_gu, NCH_GU), (wu_hbm, stg_u, wu_bf, ch_gu, NCH_GU),
            (wd_hbm, stg_d, wd_bf, ch_d, NCH_D))

    def piece_copy(m, e, c):
        src, stg, _, ch, _ = mats[m]
        return pltpu.make_async_copy(src.at[e, pl.ds(pl.multiple_of(c * ch, ch), ch), :], stg, sems.at[m])

    def stream(m, e, slot, lo, hi):
        _, stg, dst, ch, nch = mats[m]

        def step(c, carry):
            piece_copy(m, e, c).wait()
            dst[slot, pl.ds(pl.multiple_of(c * ch, ch), ch), :] = stg[...].astype(BF16)

            @pl.when(c + 1 < nch)
            def _():
                piece_copy(m, e, c + 1).start()

            return carry

        lax.fori_loop(lo, hi, step, 0)

    @pl.when(b < nused_ref[0])
    def _():
        e_next = nxt_ref[b]
        slot = slot_ref[b]
        other = 1 - slot

        @pl.when(b == 0)
        def _():
            for m in range(3):
                piece_copy(m, blk_e_ref[0], 0).start()
            for m in range(3):
                stream(m, blk_e_ref[0], slot, 0, mats[m][4])

        @pl.when(jnp.logical_and(first_ref[b] == 1, e_next >= 0))
        def _():
            for m in range(3):
                piece_copy(m, e_next, 0).start()

        def prefetch(m, lo_ref, hi_ref):
            @pl.when(e_next >= 0)
            def _():
                stream(m, e_next, other, lo_ref[b], hi_ref[b])

        x2d_scr[...] = x_ref[...].reshape(x2d_scr.shape)
        x = x2d_scr[...].astype(BF16)
        g = jnp.dot(x, wg_bf[slot], preferred_element_type=F32)
        u = jnp.dot(x, wu_bf[slot], preferred_element_type=F32)
        h = (g * jax.nn.sigmoid(g)) * u
        y = jnp.dot(h.astype(BF16), wd_bf[slot], preferred_element_type=F32)
        y_ref[...] = y.reshape(y_ref.shape)
        prefetch(0, lo_gu_ref, hi_gu_ref)
        prefetch(1, lo_gu_ref, hi_gu_ref)
        prefetch(2, lo_d_ref, hi_d_ref)

    @pl.when(b >= nused_ref[0])
    def _():
        y_ref[...] = jnp.zeros_like(y_ref)


def _stream_schedule(padded, nused, n_blocks):
    i32 = jnp.int32
    nblk = (padded // MOE_BLOCK).astype(i32)
    bend = jnp.cumsum(nblk)
    bstart = bend - nblk
    blk = jnp.minimum(jnp.arange(n_blocks, dtype=i32), nused - 1)
    blk_e = jnp.minimum(jnp.sum(bend[None, :] <= blk[:, None], axis=1), N_EXPERTS - 1).astype(i32)
    e_ids = jnp.arange(N_EXPERTS, dtype=i32)
    of_block = lambda table: jnp.sum(jnp.where(blk_e[:, None] == e_ids[None, :], table[None, :], 0), axis=1)
    r = blk - of_block(bstart)
    nr = jnp.maximum(of_block(nblk), 1)
    live = jnp.where(nblk > 0, e_ids, N_EXPERTS)
    suffix = lax.cummin(live[::-1])[::-1]
    nxt = jnp.concatenate([suffix[1:], jnp.full((1,), N_EXPERTS, i32)])
    nxt = jnp.where(nxt >= N_EXPERTS, -1, nxt)
    ordinal = jnp.cumsum((nblk > 0).astype(i32)) - 1
    first = (r == 0).astype(i32)
    slot = (of_block(ordinal) % 2).astype(i32)
    pieces = lambda nch: ((nch * r) // nr, (nch * (r + 1)) // nr)
    lo_gu, hi_gu = pieces(NCH_GU)
    lo_d, hi_d = pieces(NCH_D)
    return blk_e, first, slot, of_block(nxt).astype(i32), lo_gu.astype(i32), hi_gu.astype(i32), lo_d.astype(i32), hi_d.astype(i32)


def _experts_streamed(xs, wg, wu, wd, padded, nused):
    n_rows, _, d = xs.shape
    ff = wg.shape[2]
    nb = n_rows // MOE_BLOCK
    sched = _stream_schedule(padded, nused[0], nb)
    rows = lambda b, nu, *_: (jnp.minimum(b, nu[0] - 1), 0, 0)
    hbm = pl.BlockSpec(memory_space=pl.ANY)
    return pl.pallas_call(
        _experts_streamed_kernel,
        grid_spec=pltpu.PrefetchScalarGridSpec(
            num_scalar_prefetch=9,
            grid=(nb,),
            in_specs=[pl.BlockSpec((MOE_BLOCK, 1, d), rows), hbm, hbm, hbm],
            out_specs=pl.BlockSpec((MOE_BLOCK, 1, d), lambda b, *_: (b, 0, 0)),
            scratch_shapes=[pltpu.VMEM((2, d, ff), BF16), pltpu.VMEM((2, d, ff), BF16), pltpu.VMEM((2, ff, d), BF16),
                            pltpu.VMEM((d // NCH_GU, ff), F32), pltpu.VMEM((d // NCH_GU, ff), F32),
                            pltpu.VMEM((ff // NCH_D, d), F32),
                            pltpu.VMEM((MOE_BLOCK, d), F32),
                            pltpu.SemaphoreType.DMA((3,))]),
        out_shape=jax.ShapeDtypeStruct((n_rows, 1, d), F32),
        compiler_params=_params(("arbitrary",)),
        name="experts_routed",
    )(nused, *sched, xs, wg, wu, wd)


def _combine_kernel(dest_ref, wts_ref, ysh_ref, x1_ref, gt_ref, gpost_ref, yr_ref,
                    out_ref, buf, row2d_scr, sem):
    t = x1_ref.shape[0]

    def copy(tok, k):
        return _row_copy(yr_ref, dest_ref[0, k, tok], buf.at[k], tok, sem)

    def start(tok, c):
        for k in range(TOP_K):
            copy(tok, k).start()
        return c

    def wait(tok, c):
        for k in range(TOP_K):
            _row_copy(yr_ref, 0, buf.at[k], 0, sem).wait()
        return c

    lax.fori_loop(0, t, start, 0, unroll=ROW_DMA_UNROLL)
    lax.fori_loop(0, t, wait, 0, unroll=ROW_DMA_UNROLL)
    row2d_scr[...] = ysh_ref[...].reshape(row2d_scr.shape)
    y = row2d_scr[...]
    for k in range(TOP_K):
        row2d_scr[...] = buf[k].reshape(row2d_scr.shape)
        y = y + wts_ref[:, k:k + 1] * row2d_scr[...]
    out_ref[...] = x1_ref[...] + gt_ref[0] * (_rms(y) * gpost_ref[...])


def _combine(yr, dest3, wts, ysh, x1, gt_f, g_post, seq_len):
    n, d = x1.shape
    t = MOE_BLOCK
    tiles_per_seq = seq_len // t
    smem3 = pl.BlockSpec((1, SUBLANES, t), lambda i, *_: (i, 0, 0), memory_space=pltpu.SMEM)
    row = lambda i, *_: (i, 0)
    return pl.pallas_call(
        _combine_kernel,
        grid_spec=pltpu.PrefetchScalarGridSpec(
            num_scalar_prefetch=0,
            grid=(n // t,),
            in_specs=[smem3,
                      pl.BlockSpec((t, SUBLANES), row),
                      pl.BlockSpec((t, 1, d), lambda i, *_: (i, 0, 0)),
                      pl.BlockSpec((t, d), row),
                      pl.BlockSpec((1, 1, d), lambda i, *_: (i // tiles_per_seq, 0, 0)),
                      pl.BlockSpec((1, d), lambda i, *_: (0, 0)),
                      pl.BlockSpec(memory_space=pl.ANY)],
            out_specs=pl.BlockSpec((t, d), row),
            scratch_shapes=[pltpu.VMEM((TOP_K, t, 1, d), F32), pltpu.VMEM((t, d), F32),
                            pltpu.SemaphoreType.DMA(())]),
        out_shape=jax.ShapeDtypeStruct((n, d), F32),
        compiler_params=_params(("arbitrary",)),
        name="combine",
    )(dest3, wts, ysh, x1, gt_f, g_post.reshape(1, d), yr)


def _layer(x2, c, w_mod, b_mod, g_pre_mix, g_post_mix, g_pre_ffn, g_post_ffn, w_in, q_norm_g, k_norm_g,
           w_pool, pool_scale, w_out, w_router, router_bias, w_exp_gate, w_exp_up, w_exp_down,
           w_sh_gate, w_sh_up, w_sh_down, bsz, seq_len):
    n, d = x2.shape
    mod = _mod(c, w_mod, b_mod).reshape(bsz, N_MOD, 1, d)
    sh_a, sc_a, gt_a, sh_f, sc_f, gt_f = (mod[:, i] for i in range(N_MOD))

    qkv, pgg = _inproj(x2, g_pre_mix, sc_a, sh_a, w_in.astype(BF16), q_norm_g, k_norm_g, seq_len)
    attn = _attention(qkv, q_norm_g, k_norm_g, bsz, seq_len, d)
    x1, h2 = _mix(attn, pgg, x2, gt_a, sc_f, sh_f, w_pool.astype(BF16), pool_scale, w_out.astype(BF16),
                  g_post_mix, g_pre_ffn, seq_len)

    idx3, rank3, wts3, cnt = _router(h2, w_router, router_bias)
    counts = cnt[:, 0]
    padded = ((counts + MOE_BLOCK - 1) // MOE_BLOCK) * MOE_BLOCK
    pend = jnp.cumsum(padded)
    pstart = (pend - padded).astype(jnp.int32)
    n_rows = n * TOP_K + N_EXPERTS * MOE_BLOCK
    n_blocks = n_rows // MOE_BLOCK
    nused = (pend[-1] // MOE_BLOCK).astype(jnp.int32).reshape(1)
    all_blocks = jnp.arange(n_blocks, dtype=jnp.int32)
    padrow = jnp.concatenate([jnp.where(padded > 0, pend.astype(jnp.int32) - MOE_BLOCK, -1),
                              jnp.where(all_blocks >= nused[0], all_blocks * MOE_BLOCK, -1)])

    dest3 = rank3 + jnp.sum(jnp.where(idx3[..., None] == jnp.arange(N_EXPERTS, dtype=jnp.int32), pstart, 0), axis=-1)
    ysh, xs = _dispatch_shared(h2, dest3, padrow, n_rows, w_sh_gate.astype(BF16), w_sh_up.astype(BF16),
                               w_sh_down.astype(BF16))
    yr = _experts_streamed(xs, w_exp_gate, w_exp_up, w_exp_down, padded, nused)
    wts = wts3.transpose(0, 2, 1).reshape(n, SUBLANES)
    return _combine(yr, dest3, wts, ysh, x1, gt_f, g_post_ffn, seq_len)


def kernel(x, c, w_mod, b_mod, g_pre_mix, g_post_mix, g_pre_ffn, g_post_ffn, w_in, q_norm_g, k_norm_g, w_pool,
           pool_scale, w_out, w_router, router_bias, w_exp_gate, w_exp_up, w_exp_down, w_sh_gate, w_sh_up,
           w_sh_down):
    bsz, seq_len, d = x.shape
    x2 = x.reshape(bsz * seq_len, d)
    for l in range(w_mod.shape[0]):
        x2 = _layer(x2, c, w_mod[l], b_mod[l], g_pre_mix[l], g_post_mix[l], g_pre_ffn[l], g_post_ffn[l],
                    w_in[l], q_norm_g[l], k_norm_g[l], w_pool[l], pool_scale[l], w_out[l], w_router[l],
                    router_bias[l], w_exp_gate[l], w_exp_up[l], w_exp_down[l], w_sh_gate[l], w_sh_up[l],
                    w_sh_down[l], bsz, seq_len)
    return x2.reshape(bsz, seq_len, d)
```

```python
import functools
import math

import jax
import jax.numpy as jnp
import numpy as np
from jax import lax
from jax.experimental import pallas as pl
from jax.experimental.pallas import tpu as pltpu

F32 = jnp.float32
BF16 = jnp.bfloat16

NORM_EPS = 1e-6
HEAD_DIM = 128
KV_GROUP = 4
GRID_W = 64
ROPE_THETA = 10000.0
ROPE_AXIS_DIM = HEAD_DIM // 2
POOL_WINDOWS = (2, 4, 8, 16)
POOL_HALO = 8
N_EXPERTS = 64
TOP_K = 6
N_EXPERT_GROUPS = 8
TOPK_GROUPS = 4
EXPERTS_PER_GROUP = N_EXPERTS // N_EXPERT_GROUPS
ROUTED_SCALE = 2.5
MOE_BLOCK = 256
N_MOD = 6
LANES = 128
SUBLANES = 8
VMEM_LIMIT = 56 * 1024 * 1024

Q_PRESCALE = (HEAD_DIM ** -0.5) * math.log2(math.e)


def _rms(x):
    return x * lax.rsqrt(jnp.mean(x * x, axis=-1, keepdims=True) + NORM_EPS)


def _params(sem, vmem=VMEM_LIMIT):
    return pltpu.CompilerParams(dimension_semantics=sem, vmem_limit_bytes=vmem)


def _mod_kernel(ct_ref, w_ref, b_ref, o_ref):
    ct = ct_ref[...]
    s = ct * jax.nn.sigmoid(ct)
    w = w_ref[...]
    rows = [jnp.sum(w * s[:, b:b + 1], axis=0, keepdims=True) for b in range(ct.shape[1])]
    o_ref[...] = jnp.concatenate(rows, axis=0) + b_ref[...]


def _mod(c, w_mod, b_mod):
    bsz, d = c.shape
    n = w_mod.shape[1]
    tn = min(512, n)
    return pl.pallas_call(
        _mod_kernel,
        grid=(n // tn,),
        in_specs=[pl.BlockSpec((d, bsz), lambda j: (0, 0)),
                  pl.BlockSpec((d, tn), lambda j: (0, j)),
                  pl.BlockSpec((1, tn), lambda j: (0, j))],
        out_specs=pl.BlockSpec((bsz, tn), lambda j: (0, j)),
        out_shape=jax.ShapeDtypeStruct((bsz, n), F32),
        compiler_params=_params(("arbitrary",)),
        name="mod",
    )(c.T, w_mod, b_mod.reshape(1, n))


def _rope_tables(seq_len):
    t = np.arange(seq_len)
    row = (t // GRID_W).astype(np.float32)
    col = (t % GRID_W).astype(np.float32)
    inv_freq = (ROPE_THETA ** (-np.arange(0, ROPE_AXIS_DIM, 2, dtype=np.float32) / ROPE_AXIS_DIM)).astype(np.float32)
    ang_r = row[:, None] * inv_freq[None, :]
    ang_c = col[:, None] * inv_freq[None, :]
    cr, sr, cc, sc = np.cos(ang_r), np.sin(ang_r), np.cos(ang_c), np.sin(ang_c)
    z = np.zeros_like(sr)
    cos = np.concatenate([cr, cr, cc, cc], axis=1)
    sa = np.concatenate([-sr, z, -sc, z], axis=1)
    sb = np.concatenate([z, sr, z, sc], axis=1)
    return tuple(jnp.asarray(a, F32) for a in (cos, sa, sb))


def _inproj_kernel(x_ref, g_ref, sc_ref, sh_ref, w_ref, gq_ref, gk_ref, cos_ref, sa_ref, sb_ref,
                   qkv_ref, f32_ref, h_scr, *, nq):
    j = pl.program_id(1)

    @pl.when(j == 0)
    def _():
        h = _rms(x_ref[...]) * g_ref[...]
        h = h * (1.0 + sc_ref[0]) + sh_ref[0]
        h_scr[...] = h.astype(BF16)

    acc = jnp.dot(h_scr[...], w_ref[...], preferred_element_type=F32)
    heads = acc.shape[1] // HEAD_DIM

    def qk_epilogue(gain):
        cos, sa, sb = cos_ref[...], sa_ref[...], sb_ref[...]
        for hh in range(heads):
            sl = slice(hh * HEAD_DIM, (hh + 1) * HEAD_DIM)
            y = _rms(acc[:, sl]) * gain
            r = y * cos + pltpu.roll(y, 3 * HEAD_DIM // 4, 1) * sa + pltpu.roll(y, HEAD_DIM // 4, 1) * sb
            qkv_ref[:, sl] = r.astype(BF16)

    @pl.when(j < nq)
    def _():
        qk_epilogue(gq_ref[...] * Q_PRESCALE)

    @pl.when(j == nq)
    def _():
        qk_epilogue(gk_ref[...])

    @pl.when(j == nq + 1)
    def _():
        qkv_ref[...] = acc.astype(BF16)

    @pl.when(j > nq + 1)
    def _():
        f32_ref[...] = acc


def _inproj(x2, g_pre, sc, sh, w_in_bf, gq, gk, seq_len):
    n, d = x2.shape
    kv_w = d // KV_GROUP
    tn = kv_w
    nq = d // tn
    n_qkv = nq + 2
    n_f32 = 3 * d // tn
    tm = min(1024, seq_len)
    tiles_per_seq = seq_len // tm
    cos, sa, sb = _rope_tables(seq_len)
    row = lambda i, j: (i, 0)
    bat = lambda i, j: (i // tiles_per_seq, 0, 0)
    pos = lambda i, j: (i % tiles_per_seq, 0)
    fixed = lambda i, j: (0, 0)
    return pl.pallas_call(
        functools.partial(_inproj_kernel, nq=nq),
        grid=(n // tm, n_qkv + n_f32),
        in_specs=[pl.BlockSpec((tm, d), row),
                  pl.BlockSpec((1, d), fixed),
                  pl.BlockSpec((1, 1, d), bat),
                  pl.BlockSpec((1, 1, d), bat),
                  pl.BlockSpec((d, tn), lambda i, j: (0, j)),
                  pl.BlockSpec((1, HEAD_DIM), fixed),
                  pl.BlockSpec((1, HEAD_DIM), fixed),
                  pl.BlockSpec((tm, HEAD_DIM), pos),
                  pl.BlockSpec((tm, HEAD_DIM), pos),
                  pl.BlockSpec((tm, HEAD_DIM), pos)],
        out_specs=[pl.BlockSpec((tm, tn), lambda i, j: (i, jnp.minimum(j, n_qkv - 1))),
                   pl.BlockSpec((tm, tn), lambda i, j: (i, jnp.maximum(j - n_qkv, 0)))],
        out_shape=[jax.ShapeDtypeStruct((n, n_qkv * tn), BF16),
                   jax.ShapeDtypeStruct((n, n_f32 * tn), F32)],
        scratch_shapes=[pltpu.VMEM((tm, d), BF16)],
        compiler_params=_params(("arbitrary", "arbitrary")),
        name="inproj",
    )(x2, g_pre.reshape(1, d), sc, sh, w_in_bf, gq.reshape(1, HEAD_DIM), gk.reshape(1, HEAD_DIM),
      cos, sa, sb)


def _attn_kernel(q_ref, k_ref, v_ref, o_ref, *, tk):
    tq = q_ref.shape[0]
    seq_len = k_ref.shape[0]
    qs = jnp.concatenate([q_ref[:, g * HEAD_DIM:(g + 1) * HEAD_DIM] for g in range(KV_GROUP)], axis=0)
    rows = qs.shape[0]

    def body(c, carry):
        m, l, acc = carry
        start = pl.multiple_of(c * tk, tk)
        kc = k_ref[pl.ds(start, tk), :]
        vc = v_ref[pl.ds(start, tk), :]
        s = lax.dot_general(qs, kc, (((1,), (1,)), ((), ())), preferred_element_type=F32)
        m_new = jnp.maximum(m, jnp.max(s, axis=-1, keepdims=True))
        alpha = jnp.exp2(m - m_new)
        p = jnp.exp2(s - m_new)
        l = alpha * l + jnp.sum(p, axis=-1, keepdims=True)
        acc = alpha * acc + jnp.dot(p.astype(BF16), vc, preferred_element_type=F32)
        return m_new, l, acc

    init = (jnp.full((rows, 1), -jnp.inf, F32), jnp.zeros((rows, 1), F32), jnp.zeros((rows, HEAD_DIM), F32))
    _, l, acc = lax.fori_loop(0, seq_len // tk, body, init)
    o = acc / l
    for g in range(KV_GROUP):
        o_ref[:, g * HEAD_DIM:(g + 1) * HEAD_DIM] = o[g * tq:(g + 1) * tq].astype(BF16)


ATTN_SCORE_BUFS = 2


def _attn_bounded_kernel(q_ref, k_ref, v_ref, o_ref, *s_scr, tk):
    tq = q_ref.shape[0]
    seq_len = k_ref.shape[0]
    qs = jnp.concatenate([q_ref[:, g * HEAD_DIM:(g + 1) * HEAD_DIM] for g in range(KV_GROUP)], axis=0)
    rows = qs.shape[0]
    nchunks = seq_len // tk

    def qk(c, dst):
        kc = k_ref[c * tk:(c + 1) * tk, :]
        dst[...] = lax.dot_general(kc, qs, (((1,), (1,)), ((), ())), preferred_element_type=F32)

    def pv(c, src, l, acc):
        vc = v_ref[c * tk:(c + 1) * tk, :]
        pt = jnp.exp2(src[...])
        l = l + jnp.sum(pt.reshape(tk // SUBLANES, SUBLANES, rows), axis=0)
        acc = acc + lax.dot_general(vc, pt.astype(BF16), (((0,), (0,)), ((), ())),
                                    preferred_element_type=F32)
        return l, acc

    l, acc = jnp.zeros((SUBLANES, rows), F32), jnp.zeros((HEAD_DIM, rows), F32)
    qk(0, s_scr[0])
    for c in range(nchunks):
        if c + 1 < nchunks:
            qk(c + 1, s_scr[(c + 1) % ATTN_SCORE_BUFS])
        l, acc = pv(c, s_scr[c % ATTN_SCORE_BUFS], l, acc)
    o = (acc / jnp.sum(l, axis=0, keepdims=True)).T
    for g in range(KV_GROUP):
        o_ref[:, g * HEAD_DIM:(g + 1) * HEAD_DIM] = o[g * tq:(g + 1) * tq].astype(BF16)


ATTN_SCORE_BOUND = 60.0


def _attention(qkv, gq, gk, bsz, seq_len, d):
    n = bsz * seq_len
    n_kv = d // HEAD_DIM // KV_GROUP
    gw = KV_GROUP * HEAD_DIM
    tk = min(512, seq_len)
    k_col0 = d // HEAD_DIM
    v_col0 = k_col0 + n_kv

    def call(body, tq, n_score_bufs):
        qt = seq_len // tq
        scratch = [pltpu.VMEM((tk, KV_GROUP * tq), F32)] * n_score_bufs
        return pl.pallas_call(
            functools.partial(body, tk=tk),
            grid=(bsz, n_kv, qt),
            in_specs=[pl.BlockSpec((tq, gw), lambda b, h, i: (b * qt + i, h)),
                      pl.BlockSpec((seq_len, HEAD_DIM), lambda b, h, i: (b, k_col0 + h)),
                      pl.BlockSpec((seq_len, HEAD_DIM), lambda b, h, i: (b, v_col0 + h))],
            out_specs=pl.BlockSpec((tq, gw), lambda b, h, i: (b * qt + i, h)),
            out_shape=jax.ShapeDtypeStruct((n, d), BF16),
            scratch_shapes=scratch,
            compiler_params=_params(("arbitrary", "arbitrary", "arbitrary")),
            name="attn",
        )(qkv, qkv, qkv)

    online = lambda: call(_attn_kernel, min(128, seq_len), 0)
    bound = 1.02 * HEAD_DIM * Q_PRESCALE * jnp.max(jnp.abs(gq)) * jnp.max(jnp.abs(gk))
    bounded = lambda: call(_attn_bounded_kernel, min(256, seq_len), ATTN_SCORE_BUFS)
    return lax.cond(bound <= ATTN_SCORE_BOUND, bounded, online)


def _mix_kernel(attn_ref, p_ref, pprev_ref, pnext_ref, ga_ref, gp_ref, x_ref, gt_ref, scf_ref, shf_ref,
                wpool_ref, pscale_ref, wout_ref, gpost_ref, gpre_ref,
                x1_ref, h2_ref, pext_scr, merged_scr, *, seq_len):
    tm, d = p_ref.shape
    gw = d // len(POOL_WINDOWS)
    tiles_per_seq = seq_len // tm
    ti = pl.program_id(0) % tiles_per_seq
    pext_scr[0:POOL_HALO, :] = jnp.where(ti > 0, pprev_ref[...], 0.0)
    pext_scr[POOL_HALO:POOL_HALO + tm, :] = p_ref[...]
    pext_scr[POOL_HALO + tm:2 * POOL_HALO + tm, :] = jnp.where(ti < tiles_per_seq - 1, pnext_ref[...], 0.0)
    tpos = ti * tm + lax.broadcasted_iota(jnp.int32, (tm, 1), 0)
    for gi, w in enumerate(POOL_WINDOWS):
        cs = slice(gi * gw, (gi + 1) * gw)
        win = pext_scr[POOL_HALO - w // 2:POOL_HALO - w // 2 + tm, cs]
        for o in range(1 - w // 2, w // 2):
            win = win + pext_scr[POOL_HALO + o:POOL_HALO + o + tm, cs]
        cnt = (jnp.minimum(tpos + w // 2, seq_len) - jnp.maximum(tpos - w // 2, 0)).astype(F32)
        pooled = win / cnt - p_ref[:, cs]
        mixed = jnp.dot(pooled.astype(BF16), wpool_ref[gi], preferred_element_type=F32) * pscale_ref[:, cs]
        merged = (jax.nn.sigmoid(ga_ref[:, cs]) * attn_ref[:, cs].astype(F32)
                  + jax.nn.sigmoid(gp_ref[:, cs]) * mixed)
        merged_scr[:, cs] = merged.astype(BF16)
    y = jnp.dot(merged_scr[...], wout_ref[...], preferred_element_type=F32)
    x1 = x_ref[...] + gt_ref[0] * (_rms(y) * gpost_ref[...])
    x1_ref[...] = x1
    h2 = (_rms(x1) * gpre_ref[...]) * (1.0 + scf_ref[0]) + shf_ref[0]
    h2_ref[...] = h2.reshape(h2_ref.shape)


def _mix(attn, pgg, x2, gt_a, sc_f, sh_f, w_pool_bf, pool_scale, w_out_bf, g_post, g_pre_ffn, seq_len):
    n, d = x2.shape
    tm = min(256, seq_len)
    tiles_per_seq = seq_len // tm
    hb = tm // POOL_HALO
    n_hb = n // POOL_HALO
    row = lambda i: (i, 0)
    bat = lambda i: (i // tiles_per_seq, 0, 0)
    fixed2 = lambda i: (0, 0)
    ng = len(POOL_WINDOWS)
    gw = d // ng
    return pl.pallas_call(
        functools.partial(_mix_kernel, seq_len=seq_len),
        grid=(n // tm,),
        in_specs=[pl.BlockSpec((tm, d), row),
                  pl.BlockSpec((tm, d), lambda i: (i, 0)),
                  pl.BlockSpec((POOL_HALO, d), lambda i: (jnp.maximum(i * hb - 1, 0), 0)),
                  pl.BlockSpec((POOL_HALO, d), lambda i: (jnp.minimum((i + 1) * hb, n_hb - 1), 0)),
                  pl.BlockSpec((tm, d), lambda i: (i, 1)),
                  pl.BlockSpec((tm, d), lambda i: (i, 2)),
                  pl.BlockSpec((tm, d), row),
                  pl.BlockSpec((1, 1, d), bat),
                  pl.BlockSpec((1, 1, d), bat),
                  pl.BlockSpec((1, 1, d), bat),
                  pl.BlockSpec((ng, gw, gw), lambda i: (0, 0, 0)),
                  pl.BlockSpec((1, d), fixed2),
                  pl.BlockSpec((d, d), fixed2),
                  pl.BlockSpec((1, d), fixed2),
                  pl.BlockSpec((1, d), fixed2)],
        out_specs=[pl.BlockSpec((tm, d), row), pl.BlockSpec((tm, 1, d), lambda i: (i, 0, 0))],
        out_shape=[jax.ShapeDtypeStruct((n, d), F32), jax.ShapeDtypeStruct((n, 1, d), F32)],
        scratch_shapes=[pltpu.VMEM((tm + 2 * POOL_HALO, d), F32), pltpu.VMEM((tm, d), BF16)],
        compiler_params=_params(("arbitrary",)),
        name="mix",
    )(attn, pgg, pgg, pgg, pgg, pgg, x2, gt_a, sc_f, sh_f, w_pool_bf, pool_scale.reshape(1, d), w_out_bf,
      g_post.reshape(1, d), g_pre_ffn.reshape(1, d))


def _router_kernel(h_ref, wrt_ref, bias_ref, tri_ref, idx_ref, rank_ref, wts_ref, cnt_ref, carry_scr, h2d_scr):
    t = h_ref.shape[0]
    neg = -jnp.inf

    @pl.when(pl.program_id(0) == 0)
    def _():
        carry_scr[...] = jnp.zeros_like(carry_scr)

    h2d_scr[...] = h_ref[...].reshape(h2d_scr.shape)
    logits = lax.dot_general(wrt_ref[...], h2d_scr[...], (((1,), (1,)), ((), ())),
                             precision=lax.Precision.HIGHEST, preferred_element_type=F32)
    scores = jax.nn.sigmoid(logits)
    choice = scores + bias_ref[...]
    iota8 = lax.broadcasted_iota(jnp.int32, (SUBLANES, t), 0)
    iota_e = lax.broadcasted_iota(jnp.int32, (N_EXPERTS, t), 0)

    gs = jnp.full((N_EXPERT_GROUPS, t), neg, F32)
    for g in range(N_EXPERT_GROUPS):
        blk = choice[g * EXPERTS_PER_GROUP:(g + 1) * EXPERTS_PER_GROUP]
        m1 = jnp.max(blk, axis=0, keepdims=True)
        first = jnp.min(jnp.where(blk == m1, iota8, EXPERTS_PER_GROUP), axis=0, keepdims=True)
        m2 = jnp.max(jnp.where(iota8 == first, neg, blk), axis=0, keepdims=True)
        gs = jnp.where(iota8 == g, m1 + m2, gs)

    gmask = jnp.zeros((N_EXPERT_GROUPS, t), F32)
    for _ in range(TOPK_GROUPS):
        m = jnp.max(gs, axis=0, keepdims=True)
        first = jnp.min(jnp.where(gs == m, iota8, N_EXPERT_GROUPS), axis=0, keepdims=True)
        hit = iota8 == first
        gmask = jnp.where(hit, 1.0, gmask)
        gs = jnp.where(hit, neg, gs)

    masked = jnp.concatenate(
        [jnp.where(jnp.max(jnp.where(iota8 == g, gmask, 0.0), axis=0, keepdims=True) > 0.0,
                   choice[g * EXPERTS_PER_GROUP:(g + 1) * EXPERTS_PER_GROUP], neg)
         for g in range(N_EXPERT_GROUPS)], axis=0)

    selmask = jnp.zeros((N_EXPERTS, t), F32)
    idx_rows, sel_rows = [], []
    for _ in range(TOP_K):
        m = jnp.max(masked, axis=0, keepdims=True)
        first = jnp.min(jnp.where(masked == m, iota_e, N_EXPERTS), axis=0, keepdims=True)
        hit = iota_e == first
        idx_rows.append(first)
        sel_rows.append(jnp.sum(jnp.where(hit, scores, 0.0), axis=0, keepdims=True))
        selmask = jnp.where(hit, 1.0, selmask)
        masked = jnp.where(hit, neg, masked)
    denom = sel_rows[0]
    for r in sel_rows[1:]:
        denom = denom + r

    prefix = jnp.dot(selmask.astype(BF16), tri_ref[...], preferred_element_type=F32)
    rankfull = prefix + carry_scr[...]
    carry = carry_scr[...] + jnp.sum(selmask, axis=1, keepdims=True)
    carry_scr[...] = carry
    cnt_ref[...] = jnp.broadcast_to(carry, cnt_ref.shape).astype(jnp.int32)

    idx_out = jnp.zeros((SUBLANES, t), jnp.int32)
    rank_out = jnp.zeros((SUBLANES, t), jnp.int32)
    wts_out = jnp.zeros((SUBLANES, t), F32)
    for k in range(TOP_K):
        rk = jnp.sum(jnp.where(iota_e == idx_rows[k], rankfull, 0.0), axis=0, keepdims=True)
        idx_out = jnp.where(iota8 == k, idx_rows[k], idx_out)
        rank_out = jnp.where(iota8 == k, rk.astype(jnp.int32), rank_out)
        wts_out = jnp.where(iota8 == k, sel_rows[k] / denom * ROUTED_SCALE, wts_out)
    idx_ref[0] = idx_out
    rank_ref[0] = rank_out
    wts_ref[0] = wts_out


def _router(h2, w_router, router_bias):
    n, _, d = h2.shape
    t = MOE_BLOCK
    nt = n // t
    tri = (jnp.arange(t)[:, None] < jnp.arange(t)[None, :]).astype(BF16)
    blk3 = pl.BlockSpec((1, SUBLANES, t), lambda i: (i, 0, 0))
    return pl.pallas_call(
        _router_kernel,
        grid=(nt,),
        in_specs=[pl.BlockSpec((t, 1, d), lambda i: (i, 0, 0)),
                  pl.BlockSpec((N_EXPERTS, d), lambda i: (0, 0)),
                  pl.BlockSpec((N_EXPERTS, 1), lambda i: (0, 0)),
                  pl.BlockSpec((t, t), lambda i: (0, 0))],
        out_specs=[blk3, blk3, blk3, pl.BlockSpec((N_EXPERTS, LANES), lambda i: (0, 0))],
        out_shape=[jax.ShapeDtypeStruct((nt, SUBLANES, t), jnp.int32),
                   jax.ShapeDtypeStruct((nt, SUBLANES, t), jnp.int32),
                   jax.ShapeDtypeStruct((nt, SUBLANES, t), F32),
                   jax.ShapeDtypeStruct((N_EXPERTS, LANES), jnp.int32)],
        scratch_shapes=[pltpu.VMEM((N_EXPERTS, 1), F32), pltpu.VMEM((t, d), F32)],
        compiler_params=_params(("arbitrary",)),
        name="router",
    )(h2, w_router.T, router_bias.reshape(N_EXPERTS, 1), tri)


ROW_DMA_UNROLL = 4


def _row_copy(src_ref, src_row, dst_ref, dst_row, sem):
    return pltpu.make_async_copy(src_ref.at[pl.ds(src_row, 1)], dst_ref.at[pl.ds(dst_row, 1)], sem)


def _dispatch_shared_kernel(padrow_ref, dest_ref, h_ref, wg_ref, wu_ref, wd_ref, ysh_ref, xs_ref,
                            zero_scr, x2d_scr, sem):
    t = h_ref.shape[0]

    @pl.when(pl.program_id(0) == 0)
    def _():
        zero_scr[...] = jnp.zeros_like(zero_scr)
        n_cand = padrow_ref.shape[0]

        def zero_copy(e):
            return pltpu.make_async_copy(zero_scr, xs_ref.at[pl.ds(jnp.maximum(padrow_ref[e], 0), MOE_BLOCK)], sem)

        def start(e, c):
            @pl.when(padrow_ref[e] >= 0)
            def _():
                zero_copy(e).start()
            return c

        def wait(e, c):
            @pl.when(padrow_ref[e] >= 0)
            def _():
                zero_copy(e).wait()
            return c

        lax.fori_loop(0, n_cand, start, 0)
        lax.fori_loop(0, n_cand, wait, 0)

    def start(tok, c):
        for k in range(TOP_K):
            _row_copy(h_ref, tok, xs_ref, dest_ref[0, k, tok], sem).start(priority=k % 2)
        return c

    def wait(tok, c):
        for k in range(TOP_K):
            _row_copy(h_ref, 0, xs_ref, 0, sem).wait()
        return c

    lax.fori_loop(0, t, start, 0, unroll=ROW_DMA_UNROLL)

    x2d_scr[...] = h_ref[...].reshape(x2d_scr.shape)
    x = x2d_scr[...].astype(BF16)
    g = jnp.dot(x, wg_ref[...], preferred_element_type=F32)
    u = jnp.dot(x, wu_ref[...], preferred_element_type=F32)
    h = (g * jax.nn.sigmoid(g)) * u
    y = jnp.dot(h.astype(BF16), wd_ref[...], preferred_element_type=F32)
    ysh_ref[...] = y.reshape(ysh_ref.shape)

    lax.fori_loop(0, t, wait, 0, unroll=ROW_DMA_UNROLL)


def _dispatch_shared(h2, dest3, padrow, n_rows, wg, wu, wd):
    n, _, d = h2.shape
    ff = wg.shape[1]
    t = MOE_BLOCK
    smem3 = pl.BlockSpec((1, SUBLANES, t), lambda i, *_: (i, 0, 0), memory_space=pltpu.SMEM)
    rows = pl.BlockSpec((t, 1, d), lambda i, *_: (i, 0, 0))
    fixed = lambda shape: pl.BlockSpec(shape, lambda i, *_: (0, 0))
    return pl.pallas_call(
        _dispatch_shared_kernel,
        grid_spec=pltpu.PrefetchScalarGridSpec(
            num_scalar_prefetch=1,
            grid=(n // t,),
            in_specs=[smem3, rows, fixed((d, ff)), fixed((d, ff)), fixed((ff, d))],
            out_specs=[rows, pl.BlockSpec(memory_space=pl.ANY)],
            scratch_shapes=[pltpu.VMEM((MOE_BLOCK, 1, d), F32), pltpu.VMEM((t, d), F32),
                            pltpu.SemaphoreType.DMA(())]),
        out_shape=[jax.ShapeDtypeStruct((n, 1, d), F32), jax.ShapeDtypeStruct((n_rows, 1, d), F32)],
        compiler_params=_params(("arbitrary",)),
        name="dispatch_shared",
    )(padrow, dest3, h2, wg, wu, wd)


NCH_GU = 4
NCH_D = 4


def _experts_streamed_kernel(nused_ref, blk_e_ref, first_ref, slot_ref, nxt_ref, lo_gu_ref, hi_gu_ref, lo_d_ref,
                             hi_d_ref, x_ref, wg_hbm, wu_hbm, wd_hbm, y_ref,
                             wg_bf, wu_bf, wd_bf, stg_g, stg_u, stg_d, x2d_scr, sems):
    b = pl.program_id(0)
    d, ff = wg_bf.shape[1], wg_bf.shape[2]
    ch_gu, ch_d = d // NCH_GU, ff // NCH_D
    mats = ((wg_hbm, stg_g, wg_bf, ch_gu, NCH_GU), (wu_hbm, stg_u, wu_bf, ch_gu, NCH_GU),
            (wd_hbm, stg_d, wd_bf, ch_d, NCH_D))

    def piece_copy(m, e, c):
        src, stg, _, ch, _ = mats[m]
        return pltpu.make_async_copy(src.at[e, pl.ds(pl.multiple_of(c * ch, ch), ch), :], stg, sems.at[m])

    def stream(m, e, slot, lo, hi):
        _, stg, dst, ch, nch = mats[m]

        def step(c, carry):
            piece_copy(m, e, c).wait()
            dst[slot, pl.ds(pl.multiple_of(c * ch, ch), ch), :] = stg[...].astype(BF16)

            @pl.when(c + 1 < nch)
            def _():
                piece_copy(m, e, c + 1).start()

            return carry

        lax.fori_loop(lo, hi, step, 0)

    @pl.when(b < nused_ref[0])
    def _():
        e_next = nxt_ref[b]
        slot = slot_ref[b]
        other = 1 - slot

        @pl.when(b == 0)
        def _():
            for m in range(3):
                piece_copy(m, blk_e_ref[0], 0).start()
            for m in range(3):
                stream(m, blk_e_ref[0], slot, 0, mats[m][4])

        @pl.when(jnp.logical_and(first_ref[b] == 1, e_next >= 0))
        def _():
            for m in range(3):
                piece_copy(m, e_next, 0).start()

        def prefetch(m, lo_ref, hi_ref):
            @pl.when(e_next >= 0)
            def _():
                stream(m, e_next, other, lo_ref[b], hi_ref[b])

        x2d_scr[...] = x_ref[...].reshape(x2d_scr.shape)
        x = x2d_scr[...].astype(BF16)
        g = jnp.dot(x, wg_bf[slot], preferred_element_type=F32)
        u = jnp.dot(x, wu_bf[slot], preferred_element_type=F32)
        h = (g * jax.nn.sigmoid(g)) * u
        y = jnp.dot(h.astype(BF16), wd_bf[slot], preferred_element_type=F32)
        y_ref[...] = y.reshape(y_ref.shape)
        prefetch(0, lo_gu_ref, hi_gu_ref)
        prefetch(1, lo_gu_ref, hi_gu_ref)
        prefetch(2, lo_d_ref, hi_d_ref)

    @pl.when(b >= nused_ref[0])
    def _():
        y_ref[...] = jnp.zeros_like(y_ref)


def _stream_schedule(padded, nused, n_blocks):
    i32 = jnp.int32
    nblk = (padded // MOE_BLOCK).astype(i32)
    bend = jnp.cumsum(nblk)
    bstart = bend - nblk
    blk = jnp.minimum(jnp.arange(n_blocks, dtype=i32), nused - 1)
    blk_e = jnp.minimum(jnp.sum(bend[None, :] <= blk[:, None], axis=1), N_EXPERTS - 1).astype(i32)
    e_ids = jnp.arange(N_EXPERTS, dtype=i32)
    of_block = lambda table: jnp.sum(jnp.where(blk_e[:, None] == e_ids[None, :], table[None, :], 0), axis=1)
    r = blk - of_block(bstart)
    nr = jnp.maximum(of_block(nblk), 1)
    live = jnp.where(nblk > 0, e_ids, N_EXPERTS)
    suffix = lax.cummin(live[::-1])[::-1]
    nxt = jnp.concatenate([suffix[1:], jnp.full((1,), N_EXPERTS, i32)])
    nxt = jnp.where(nxt >= N_EXPERTS, -1, nxt)
    ordinal = jnp.cumsum((nblk > 0).astype(i32)) - 1
    first = (r == 0).astype(i32)
    slot = (of_block(ordinal) % 2).astype(i32)
    pieces = lambda nch: ((nch * r) // nr, (nch * (r + 1)) // nr)
    lo_gu, hi_gu = pieces(NCH_GU)
    lo_d, hi_d = pieces(NCH_D)
    return blk_e, first, slot, of_block(nxt).astype(i32), lo_gu.astype(i32), hi_gu.astype(i32), lo_d.astype(i32), hi_d.astype(i32)


def _experts_streamed(xs, wg, wu, wd, padded, nused):
    n_rows, _, d = xs.shape
    ff = wg.shape[2]
    nb = n_rows // MOE_BLOCK
    sched = _stream_schedule(padded, nused[0], nb)
    rows = lambda b, nu, *_: (jnp.minimum(b, nu[0] - 1), 0, 0)
    hbm = pl.BlockSpec(memory_space=pl.ANY)
    return pl.pallas_call(
        _experts_streamed_kernel,
        grid_spec=pltpu.PrefetchScalarGridSpec(
            num_scalar_prefetch=9,
            grid=(nb,),
            in_specs=[pl.BlockSpec((MOE_BLOCK, 1, d), rows), hbm, hbm, hbm],
            out_specs=pl.BlockSpec((MOE_BLOCK, 1, d), lambda b, *_: (b, 0, 0)),
            scratch_shapes=[pltpu.VMEM((2, d, ff), BF16), pltpu.VMEM((2, d, ff), BF16), pltpu.VMEM((2, ff, d), BF16),
                            pltpu.VMEM((d // NCH_GU, ff), F32), pltpu.VMEM((d // NCH_GU, ff), F32),
                            pltpu.VMEM((ff // NCH_D, d), F32),
                            pltpu.VMEM((MOE_BLOCK, d), F32),
                            pltpu.SemaphoreType.DMA((3,))]),
        out_shape=jax.ShapeDtypeStruct((n_rows, 1, d), F32),
        compiler_params=_params(("arbitrary",)),
        name="experts_routed",
    )(nused, *sched, xs, wg, wu, wd)


def _combine_kernel(dest_ref, wts_ref, ysh_ref, x1_ref, gt_ref, gpost_ref, yr_ref,
                    out_ref, buf, row2d_scr, sem):
    t = x1_ref.shape[0]

    def copy(tok, k):
        return _row_copy(yr_ref, dest_ref[0, k, tok], buf.at[k], tok, sem)

    def start(tok, c):
        for k in range(TOP_K):
            copy(tok, k).start(priority=k % 2)
        return c

    def wait(tok, c):
        for k in range(TOP_K):
            _row_copy(yr_ref, 0, buf.at[k], 0, sem).wait()
        return c

    lax.fori_loop(0, t, start, 0, unroll=ROW_DMA_UNROLL)
    lax.fori_loop(0, t, wait, 0, unroll=ROW_DMA_UNROLL)
    row2d_scr[...] = ysh_ref[...].reshape(row2d_scr.shape)
    y = row2d_scr[...]
    for k in range(TOP_K):
        row2d_scr[...] = buf[k].reshape(row2d_scr.shape)
        y = y + wts_ref[:, k:k + 1] * row2d_scr[...]
    out_ref[...] = x1_ref[...] + gt_ref[0] * (_rms(y) * gpost_ref[...])


def _combine(yr, dest3, wts, ysh, x1, gt_f, g_post, seq_len):
    n, d = x1.shape
    t = MOE_BLOCK
    tiles_per_seq = seq_len // t
    smem3 = pl.BlockSpec((1, SUBLANES, t), lambda i, *_: (i, 0, 0), memory_space=pltpu.SMEM)
    row = lambda i, *_: (i, 0)
    return pl.pallas_call(
        _combine_kernel,
        grid_spec=pltpu.PrefetchScalarGridSpec(
            num_scalar_prefetch=0,
            grid=(n // t,),
            in_specs=[smem3,
                      pl.BlockSpec((t, SUBLANES), row),
                      pl.BlockSpec((t, 1, d), lambda i, *_: (i, 0, 0)),
                      pl.BlockSpec((t, d), row),
                      pl.BlockSpec((1, 1, d), lambda i, *_: (i // tiles_per_seq, 0, 0)),
                      pl.BlockSpec((1, d), lambda i, *_: (0, 0)),
                      pl.BlockSpec(memory_space=pl.ANY)],
            out_specs=pl.BlockSpec((t, d), row),
            scratch_shapes=[pltpu.VMEM((TOP_K, t, 1, d), F32), pltpu.VMEM((t, d), F32),
                            pltpu.SemaphoreType.DMA(())]),
        out_shape=jax.ShapeDtypeStruct((n, d), F32),
        compiler_params=_params(("arbitrary",)),
        name="combine",
    )(dest3, wts, ysh, x1, gt_f, g_post.reshape(1, d), yr)


def _layer(x2, c, w_mod, b_mod, g_pre_mix, g_post_mix, g_pre_ffn, g_post_ffn, w_in, q_norm_g, k_norm_g,
           w_pool, pool_scale, w_out, w_router, router_bias, w_exp_gate, w_exp_up, w_exp_down,
           w_sh_gate, w_sh_up, w_sh_down, bsz, seq_len):
    n, d = x2.shape
    mod = _mod(c, w_mod, b_mod).reshape(bsz, N_MOD, 1, d)
    sh_a, sc_a, gt_a, sh_f, sc_f, gt_f = (mod[:, i] for i in range(N_MOD))

    qkv, pgg = _inproj(x2, g_pre_mix, sc_a, sh_a, w_in.astype(BF16), q_norm_g, k_norm_g, seq_len)
    attn = _attention(qkv, q_norm_g, k_norm_g, bsz, seq_len, d)
    x1, h2 = _mix(attn, pgg, x2, gt_a, sc_f, sh_f, w_pool.astype(BF16), pool_scale, w_out.astype(BF16),
                  g_post_mix, g_pre_ffn, seq_len)

    idx3, rank3, wts3, cnt = _router(h2, w_router, router_bias)
    counts = cnt[:, 0]
    padded = ((counts + MOE_BLOCK - 1) // MOE_BLOCK) * MOE_BLOCK
    pend = jnp.cumsum(padded)
    pstart = (pend - padded).astype(jnp.int32)
    n_rows = n * TOP_K + N_EXPERTS * MOE_BLOCK
    n_blocks = n_rows // MOE_BLOCK
    nused = (pend[-1] // MOE_BLOCK).astype(jnp.int32).reshape(1)
    all_blocks = jnp.arange(n_blocks, dtype=jnp.int32)
    padrow = jnp.concatenate([jnp.where(padded > 0, pend.astype(jnp.int32) - MOE_BLOCK, -1),
                              jnp.where(all_blocks >= nused[0], all_blocks * MOE_BLOCK, -1)])

    dest3 = rank3 + jnp.sum(jnp.where(idx3[..., None] == jnp.arange(N_EXPERTS, dtype=jnp.int32), pstart, 0), axis=-1)
    ysh, xs = _dispatch_shared(h2, dest3, padrow, n_rows, w_sh_gate.astype(BF16), w_sh_up.astype(BF16),
                               w_sh_down.astype(BF16))
    yr = _experts_streamed(xs, w_exp_gate, w_exp_up, w_exp_down, padded, nused)
    wts = wts3.transpose(0, 2, 1).reshape(n, SUBLANES)
    return _combine(yr, dest3, wts, ysh, x1, gt_f, g_post_ffn, seq_len)


def kernel(x, c, w_mod, b_mod, g_pre_mix, g_post_mix, g_pre_ffn, g_post_ffn, w_in, q_norm_g, k_norm_g, w_pool,
           pool_scale, w_out, w_router, router_bias, w_exp_gate, w_exp_up, w_exp_down, w_sh_gate, w_sh_up,
           w_sh_down):
    bsz, seq_len, d = x.shape
    x2 = x.reshape(bsz * seq_len, d)
    for l in range(w_mod.shape[0]):
        x2 = _layer(x2, c, w_mod[l], b_mod[l], g_pre_mix[l], g_post_mix[l], g_pre_ffn[l], g_post_ffn[l],
                    w_in[l], q_norm_g[l], k_norm_g[l], w_pool[l], pool_scale[l], w_out[l], w_router[l],
                    router_bias[l], w_exp_gate[l], w_exp_up[l], w_exp_down[l], w_sh_gate[l], w_sh_up[l],
                    w_sh_down[l], bsz, seq_len)
    return x2.reshape(bsz, seq_len, d)
```
